```python
import jax, jax.numpy as jnp
from jax import lax
import numpy as np

D_MODEL = 2048
BATCH = 4
SEQ = 8192
DEPTH = 1
DEC_BATCH = 16
DEC_SEQ = 16
PAST_LEN = 1024

CHUNK = 64
HEAD_DIM = 128
N_HEADS_A = 8
N_IDX_HEADS = 16
IDX_DIM = 64
TOPK_MAX = 256
QBLOCK = 128
N_HEADS_B = 8
BAND_CHUNKS = 8
BAND_PAST = BAND_CHUNKS * CHUNK
REL_CLIP = 2 * CHUNK
D_FF = 5632
CONV_W = 3
EPS = 1e-6
NEG = -1e30

W_A = N_HEADS_A * HEAD_DIM
W_B = N_HEADS_B * HEAD_DIM
IN_WIDTHS = (W_A, W_A, W_A, N_IDX_HEADS * IDX_DIM, IDX_DIM, N_IDX_HEADS, W_B, W_B, W_B)
IN_OFFSETS = tuple(int(o) for o in np.cumsum(IN_WIDTHS)[:-1])
D_IN = int(sum(IN_WIDTHS))

kernel_name = "hybrid_stream_dsa_chunkband_convffn_step"


def rmsnorm(x, g):
    xf = x.astype(jnp.float32)
    y = xf * lax.rsqrt(jnp.mean(xf * xf, axis=-1, keepdims=True) + EPS)
    return (y * g.astype(jnp.float32)).astype(x.dtype)


def modulate(h, shift, scale):
    return h * (1 + scale[:, None, :]) + shift[:, None, :]


def alibi_slopes(n):
    return 2.0 ** (-8.0 * jnp.arange(1, n + 1, dtype=jnp.float32) / n)


def dsa_attend(q, qi, wi, q_pos, k, v, ki, k_pos, topk, slopes):
    rel = jnp.einsum("bqhd,bld->bqhl", qi, ki).astype(jnp.float32) * (IDX_DIM ** -0.5)
    score = jnp.einsum("bqh,bqhl->bql", wi.astype(jnp.float32) * (N_IDX_HEADS ** -0.5), jax.nn.relu(rel))
    admissible = (k_pos[None, :] // CHUNK) <= (q_pos[:, None] // CHUNK)
    score = jnp.where(admissible[None], score, NEG)
    _, idx = lax.top_k(score, topk)
    k_sel = jax.vmap(lambda kk, ii: kk[ii])(k, idx)
    v_sel = jax.vmap(lambda vv, ii: vv[ii])(v, idx)
    s_pos = k_pos[idx]
    ok = (s_pos // CHUNK) <= (q_pos[None, :, None] // CHUNK)
    logits = jnp.einsum("bqhd,bqkhd->bhqk", q, k_sel).astype(jnp.float32) * (HEAD_DIM ** -0.5)
    dist = jnp.abs(q_pos[None, :, None] - s_pos).astype(jnp.float32)
    logits = logits - slopes[None, :, None, None] * dist[:, None]
    logits = jnp.where(ok[:, None], logits, NEG)
    p = jax.nn.softmax(logits, axis=-1).astype(v.dtype)
    return jnp.einsum("bhqk,bqkhd->bqhd", p, v_sel)


def mixer_a_prompt(q, qi, wi, k, v, ki, slopes):
    B, S, H, hd = q.shape
    pos = jnp.arange(S, dtype=jnp.int32)
    topk = min(TOPK_MAX, S // 4)

    def block(start):
        sl = lambda t: lax.dynamic_slice_in_dim(t, start, QBLOCK, axis=1)
        q_pos = lax.dynamic_slice_in_dim(pos, start, QBLOCK, axis=0)
        return dsa_attend(sl(q), sl(qi), sl(wi), q_pos, k, v, ki, pos, topk, slopes)

    out = lax.map(block, jnp.arange(S // QBLOCK, dtype=jnp.int32) * QBLOCK)
    return out.transpose(1, 0, 2, 3, 4).reshape(B, S, H * hd)


def mixer_a_sample(q, qi, wi, k, v, ki, ck, cv, cki, slopes):
    B, T, H, hd = q.shape
    past = ck.shape[1]
    k_all = jnp.concatenate([ck, k], axis=1)
    v_all = jnp.concatenate([cv, v], axis=1)
    ki_all = jnp.concatenate([cki, ki], axis=1)
    k_pos = jnp.arange(past + T, dtype=jnp.int32)
    q_pos = past + jnp.arange(T, dtype=jnp.int32)
    topk = min(TOPK_MAX, (past + T) // 4)
    out = dsa_attend(q, qi, wi, q_pos, k_all, v_all, ki_all, k_pos, topk, slopes)
    return out.reshape(B, T, H * hd)


def band_attend(q, k, v, q_pos, k_pos, rel_bias):
    logits = jnp.einsum("bqhd,bkhd->bhqk", q, k).astype(jnp.float32) * (HEAD_DIM ** -0.5)
    d = q_pos[:, None] - k_pos[None, :]
    bias = rel_bias[:, jnp.clip(d, -REL_CLIP, REL_CLIP) + REL_CLIP].astype(jnp.float32)
    qc = q_pos[:, None] // CHUNK
    kc = k_pos[None, :] // CHUNK
    ok = (k_pos[None, :] >= 0) & (kc <= qc) & (kc >= qc - BAND_CHUNKS)
    logits = jnp.where(ok[None, None], logits + bias[None], NEG)
    p = jax.nn.softmax(logits, axis=-1).astype(v.dtype)
    return jnp.einsum("bhqk,bkhd->bqhd", p, v)


def mixer_b_prompt(q, k, v, rel_bias):
    B, S, H, hd = q.shape
    pad = jnp.zeros((B, BAND_PAST, H, hd), k.dtype)
    kp = jnp.concatenate([pad, k], axis=1)
    vp = jnp.concatenate([pad, v], axis=1)
    band = BAND_PAST + CHUNK

    def chunk(c):
        start = c * CHUNK
        qc = lax.dynamic_slice_in_dim(q, start, CHUNK, axis=1)
        kb = lax.dynamic_slice_in_dim(kp, start, band, axis=1)
        vb = lax.dynamic_slice_in_dim(vp, start, band, axis=1)
        q_pos = start + jnp.arange(CHUNK, dtype=jnp.int32)
        k_pos = start - BAND_PAST + jnp.arange(band, dtype=jnp.int32)
        return band_attend(qc, kb, vb, q_pos, k_pos, rel_bias)

    out = lax.map(chunk, jnp.arange(S // CHUNK, dtype=jnp.int32))
    return out.transpose(1, 0, 2, 3, 4).reshape(B, S, H * hd)


def mixer_b_sample(q, k, v, cbk, cbv, rel_bias):
    B, T, H, hd = q.shape
    rows = cbk.shape[1]
    k_all = jnp.concatenate([cbk, k], axis=1)
    v_all = jnp.concatenate([cbv, v], axis=1)
    k_pos = PAST_LEN - rows + jnp.arange(rows + T, dtype=jnp.int32)
    q_pos = PAST_LEN + jnp.arange(T, dtype=jnp.int32)
    return band_attend(q, k_all, v_all, q_pos, k_pos, rel_bias).reshape(B, T, H * hd)


def causal_dwconv(a, prev, w, b):
    T = a.shape[1]
    ap = jnp.concatenate([prev, a], axis=1)
    out = b + sum(ap[:, j:j + T] * w[j] for j in range(CONV_W))
    return out, ap[:, -(CONV_W - 1):]


def trunk_layer(x, c, cache, w_ada, b_ada, norm1_g, w_in, rel_bias, w_gate, b_gate,
                w_proj_a, w_proj_b, w_out, norm2_g, w_up, w_conv, b_conv, w_down):
    B, T, _ = x.shape
    mod = jnp.dot(jax.nn.silu(c), w_ada) + b_ada
    sh1, sc1, g1, sh2, sc2, g2 = jnp.split(mod, 6, axis=-1)
    h = modulate(rmsnorm(x, norm1_g), sh1, sc1)
    qa, ka, va, qi, ki, wi, qb, kb, vb = jnp.split(h @ w_in, IN_OFFSETS, axis=-1)
    qa = qa.reshape(B, T, N_HEADS_A, HEAD_DIM)
    ka = ka.reshape(B, T, N_HEADS_A, HEAD_DIM)
    va = va.reshape(B, T, N_HEADS_A, HEAD_DIM)
    qi = qi.reshape(B, T, N_IDX_HEADS, IDX_DIM)
    qb = qb.reshape(B, T, N_HEADS_B, HEAD_DIM)
    kb = kb.reshape(B, T, N_HEADS_B, HEAD_DIM)
    vb = vb.reshape(B, T, N_HEADS_B, HEAD_DIM)
    slopes = alibi_slopes(N_HEADS_A)
    if cache is None:
        o_a = mixer_a_prompt(qa, qi, wi, ka, va, ki, slopes)
        o_b = mixer_b_prompt(qb, kb, vb, rel_bias)
        conv_prev = jnp.zeros((B, CONV_W - 1, D_FF), x.dtype)
        rows_b = min(BAND_PAST, T)
        new_bk, new_bv = kb[:, T - rows_b:], vb[:, T - rows_b:]
    else:
        ck, cv, cki, cbk, cbv, conv_prev = cache
        o_a = mixer_a_sample(qa, qi, wi, ka, va, ki, ck, cv, cki, slopes)
        o_b = mixer_b_sample(qb, kb, vb, cbk, cbv, rel_bias)
        new_bk, new_bv = kb, vb
    g_a, g_b = jnp.split(jax.nn.sigmoid(h @ w_gate + b_gate), 2, axis=-1)
    merged = g_a * (o_a @ w_proj_a) + g_b * (o_b @ w_proj_b)
    x = x + g1[:, None, :] * (merged @ w_out)
    h2 = modulate(rmsnorm(x, norm2_g), sh2, sc2)
    a, bgate = jnp.split(h2 @ w_up, 2, axis=-1)
    a, conv_state = causal_dwconv(a, conv_prev, w_conv, b_conv)
    x = x + g2[:, None, :] * ((jax.nn.silu(a) * bgate) @ w_down)
    return x, (ka, va, ki, new_bk, new_bv, conv_state)


def setup_inputs(seed: int = 0) -> dict:
    key = jax.random.key(seed)
    ks = jax.random.split(key, 32)
    rows_b = min(BAND_PAST, PAST_LEN)
    nrm = lambda k, shape, s: jax.random.normal(k, shape, jnp.float32) * s
    D = D_MODEL
    return {
        "x_prompt": nrm(ks[0], (BATCH, SEQ, D), 1.0),
        "x_sample": nrm(ks[1], (DEC_BATCH, DEC_SEQ, D), 1.0),
        "cache_a_k": nrm(ks[2], (DEPTH, DEC_BATCH, PAST_LEN, N_HEADS_A, HEAD_DIM), 1.0),
        "cache_a_v": nrm(ks[3], (DEPTH, DEC_BATCH, PAST_LEN, N_HEADS_A, HEAD_DIM), 1.0),
        "cache_idx_k": nrm(ks[4], (DEPTH, DEC_BATCH, PAST_LEN, IDX_DIM), 1.0),
        "cache_b_k": nrm(ks[5], (DEPTH, DEC_BATCH, rows_b, N_HEADS_B, HEAD_DIM), 1.0),
        "cache_b_v": nrm(ks[6], (DEPTH, DEC_BATCH, rows_b, N_HEADS_B, HEAD_DIM), 1.0),
        "state_ffn_conv": nrm(ks[7], (DEPTH, DEC_BATCH, CONV_W - 1, D_FF), 1.0),
        "c_prompt": nrm(ks[8], (BATCH, D), 1.0),
        "c_sample": nrm(ks[9], (DEC_BATCH, D), 1.0),
        "w_ada": nrm(ks[10], (DEPTH, D, 6 * D), 0.5 * D ** -0.5),
        "b_ada": nrm(ks[11], (DEPTH, 6 * D), 0.01),
        "norm1_g": 1.0 + nrm(ks[12], (DEPTH, D), 0.02),
        "w_in": nrm(ks[13], (DEPTH, D, D_IN), D ** -0.5),
        "rel_bias": nrm(ks[14], (DEPTH, N_HEADS_B, 2 * REL_CLIP + 1), 0.1),
        "w_gate": nrm(ks[15], (DEPTH, D, 2 * D), D ** -0.5),
        "b_gate": nrm(ks[16], (DEPTH, 2 * D), 0.01),
        "w_proj_a": nrm(ks[17], (DEPTH, W_A, D), W_A ** -0.5),
        "w_proj_b": nrm(ks[18], (DEPTH, W_B, D), W_B ** -0.5),
        "w_out": nrm(ks[19], (DEPTH, D, D), D ** -0.5),
        "norm2_g": 1.0 + nrm(ks[20], (DEPTH, D), 0.02),
        "w_up": nrm(ks[21], (DEPTH, D, 2 * D_FF), D ** -0.5),
        "w_conv": nrm(ks[22], (DEPTH, CONV_W, D_FF), CONV_W ** -0.5),
        "b_conv": nrm(ks[23], (DEPTH, D_FF), 0.01),
        "w_down": nrm(ks[24], (DEPTH, D_FF, D), D_FF ** -0.5),
        "final_g": 1.0 + nrm(ks[25], (D,), 0.02),
    }


def reference(x_prompt, x_sample, cache_a_k, cache_a_v, cache_idx_k, cache_b_k, cache_b_v,
              state_ffn_conv, c_prompt, c_sample, w_ada, b_ada, norm1_g, w_in, rel_bias,
              w_gate, b_gate, w_proj_a, w_proj_b, w_out, norm2_g, w_up, w_conv, b_conv,
              w_down, final_g):
    xp, xs = x_prompt, x_sample
    states_p, states_s = [], []
    for l in range(DEPTH):
        wl = (w_ada[l], b_ada[l], norm1_g[l], w_in[l], rel_bias[l], w_gate[l], b_gate[l],
              w_proj_a[l], w_proj_b[l], w_out[l], norm2_g[l], w_up[l], w_conv[l], b_conv[l],
              w_down[l])
        xp, st_p = trunk_layer(xp, c_prompt, None, *wl)
        states_p.append(st_p)
        cache_l = (cache_a_k[l], cache_a_v[l], cache_idx_k[l], cache_b_k[l], cache_b_v[l],
                   state_ffn_conv[l])
        xs, st_s = trunk_layer(xs, c_sample, cache_l, *wl)
        states_s.append(st_s)
    new_a_k_p, new_a_v_p, new_idx_k_p, new_b_k_p, new_b_v_p, new_conv_p = [jnp.stack(t) for t in zip(*states_p)]
    new_a_k_s, new_a_v_s, new_idx_k_s, new_b_k_s, new_b_v_s, new_conv_s = [jnp.stack(t) for t in zip(*states_s)]
    y_prompt = rmsnorm(xp, final_g)
    y_sample = rmsnorm(xs, final_g)
    return (y_prompt, y_sample, new_a_k_p, new_a_v_p, new_idx_k_p, new_b_k_p, new_b_v_p, new_conv_p,
            new_a_k_s, new_a_v_s, new_idx_k_s, new_b_k_s, new_b_v_s, new_conv_s)
```

```python
import functools

import jax
import jax.numpy as jnp
import numpy as np
from jax import lax
from jax.experimental import pallas as pl
from jax.experimental.pallas import tpu as pltpu

F32 = jnp.float32
BF16 = jnp.bfloat16

CHUNK = 64
CHUNK_SHIFT = 6
HEAD_DIM = 128
N_HEADS_A = 8
N_IDX_HEADS = 16
IDX_DIM = 64
TOPK_MAX = 256
N_HEADS_B = 8
BAND_CHUNKS = 8
BAND_PAST = BAND_CHUNKS * CHUNK
REL_CLIP = 2 * CHUNK
CONV_W = 3
EPS = 1e-6
NEG = -1e30
W_A = N_HEADS_A * HEAD_DIM
W_B = N_HEADS_B * HEAD_DIM
ATTN_SCALE = HEAD_DIM ** -0.5
INT_MIN = -(2 ** 31)
LANES = 128
VMEM_LIMIT = 56 * 1024 * 1024


def _cp(*sem):
    return pltpu.CompilerParams(dimension_semantics=sem, vmem_limit_bytes=VMEM_LIMIT)


def _pick(dim, pref, align):
    t = min(pref, dim)
    t -= t % align
    while t >= align:
        if dim % t == 0:
            return t
        t -= align
    return dim


def _ada_kernel(c_ref, w_ref, b_ref, o_ref):
    c = c_ref[...]
    s = c * jax.nn.sigmoid(c)
    o_ref[...] = jnp.dot(s, w_ref[...], preferred_element_type=F32,
                         precision=lax.Precision.HIGHEST) + b_ref[...]


def _ada(c, w, b):
    r, d = c.shape
    n = w.shape[1]
    tn = _pick(n, 1024, LANES)
    return pl.pallas_call(
        _ada_kernel,
        grid=(n // tn,),
        in_specs=[pl.BlockSpec((r, d), lambda j: (0, 0)),
                  pl.BlockSpec((d, tn), lambda j: (0, j)),
                  pl.BlockSpec((1, tn), lambda j: (0, j))],
        out_specs=pl.BlockSpec((r, tn), lambda j: (0, j)),
        out_shape=jax.ShapeDtypeStruct((r, n), F32),
        compiler_params=_cp("arbitrary"),
        name="ada",
    )(c, w, b.reshape(1, n))


def _rms_mod(x, g, shift, scale):
    y = x * lax.rsqrt(jnp.mean(x * x, axis=-1, keepdims=True) + EPS) * g
    return y * (1.0 + scale) + shift


def _split_bf16(x):
    hi = x.astype(BF16)
    return hi, (x - hi.astype(F32)).astype(BF16)


def _norm_mod_kernel(x_ref, g_ref, mod_ref, hi_ref, lo_ref):
    h = _rms_mod(x_ref[0], g_ref[...], mod_ref[0, 0:1, :], mod_ref[0, 1:2, :])
    hi, lo = _split_bf16(h)
    hi_ref[0] = hi
    lo_ref[0] = lo


def _norm_mod(x, g, mod):
    b, t, d = x.shape
    tm = _pick(t, 512, 16)
    row = lambda i, j: (i, j, 0)
    return pl.pallas_call(
        _norm_mod_kernel,
        grid=(b, t // tm),
        in_specs=[pl.BlockSpec((1, tm, d), row),
                  pl.BlockSpec((1, d), lambda i, j: (0, 0)),
                  pl.BlockSpec((1, 6, d), lambda i, j: (i, 0, 0))],
        out_specs=[pl.BlockSpec((1, tm, d), row), pl.BlockSpec((1, tm, d), row)],
        out_shape=[jax.ShapeDtypeStruct((b, t, d), BF16), jax.ShapeDtypeStruct((b, t, d), BF16)],
        compiler_params=_cp("arbitrary", "arbitrary"),
        name="norm_mod",
    )(x, g.reshape(1, d), mod)


def _mm_kernel(a_ref, w_ref, *o_refs):
    acc = jnp.dot(a_ref[...], w_ref[...], preferred_element_type=F32)
    for o_ref in o_refs:
        o_ref[...] = acc.astype(o_ref.dtype)


def _mm(a, w, dtypes):
    m, k = a.shape
    n = w.shape[1]
    tm = _pick(m, 1024, 16)
    tn = _pick(n, 1024, LANES)
    outs = pl.pallas_call(
        _mm_kernel,
        grid=(m // tm, n // tn),
        in_specs=[pl.BlockSpec((tm, k), lambda i, j: (i, 0)),
                  pl.BlockSpec((k, tn), lambda i, j: (0, j))],
        out_specs=[pl.BlockSpec((tm, tn), lambda i, j: (i, j)) for _ in dtypes],
        out_shape=[jax.ShapeDtypeStruct((m, n), dt) for dt in dtypes],
        compiler_params=_cp("arbitrary", "arbitrary"),
        name="proj",
    )(a, w)
    return outs


def _mm_split_kernel(ah_ref, al_ref, wh_ref, wl_ref, o_ref):
    ah = ah_ref[...]
    acc = jnp.dot(ah, wh_ref[...], preferred_element_type=F32)
    acc = acc + jnp.dot(al_ref[...], wh_ref[...], preferred_element_type=F32)
    acc = acc + jnp.dot(ah, wl_ref[...], preferred_element_type=F32)
    o_ref[...] = acc


def _mm_split(a_hi, a_lo, w_hi, w_lo):
    m, k = a_hi.shape
    n = w_hi.shape[1]
    tm = _pick(m, 1024, 16)
    tn = _pick(n, 1024, LANES)
    a_spec = pl.BlockSpec((tm, k), lambda i, j: (i, 0))
    w_spec = pl.BlockSpec((k, tn), lambda i, j: (0, j))
    return pl.pallas_call(
        _mm_split_kernel,
        grid=(m // tm, n // tn),
        in_specs=[a_spec, a_spec, w_spec, w_spec],
        out_specs=pl.BlockSpec((tm, tn), lambda i, j: (i, j)),
        out_shape=jax.ShapeDtypeStruct((m, n), F32),
        compiler_params=_cp("arbitrary", "arbitrary"),
        name="proj_split",
    )(a_hi, a_lo, w_hi, w_lo)


def _mixer_a_kernel(qa_ref, qi_ref, wi_ref, kit_ref, k_ref, v_ref, o_ref,
                    key_sc, thr_sc, m_sc, l_sc, acc_sc, qx_sc,
                    *, tq, tk, nkb, past, valid_len, topk):
    i = pl.program_id(1)
    j = pl.program_id(2)
    q0 = past + i * tq
    kend = jnp.minimum(((q0 + tq + CHUNK - 1) // CHUNK) * CHUNK, valid_len)
    nblk = (kend + tk - 1) // tk

    def admissible(jb):
        qpos = q0 + lax.broadcasted_iota(jnp.int32, (tq, tk), 0)
        kpos = jb * tk + lax.broadcasted_iota(jnp.int32, (tq, tk), 1)
        adm = (jnp.right_shift(kpos, CHUNK_SHIFT) <= jnp.right_shift(qpos, CHUNK_SHIFT)) & (kpos < valid_len)
        return adm, qpos, kpos

    @pl.when(j == 0)
    def _index_and_threshold():
        w = wi_ref[0]

        lane = lax.broadcasted_iota(jnp.int32, (tq, LANES), 1)
        for p in range(N_IDX_HEADS // 2):
            slab = qi_ref[0, :, p * LANES:(p + 1) * LANES]
            rot = pltpu.roll(slab, IDX_DIM, axis=1)
            first = lane < IDX_DIM
            for e, dup in enumerate((jnp.where(first, slab, rot), jnp.where(first, rot, slab))):
                hi, lo = _split_bf16(dup)
                qx_sc[2 * p + e, :, 0:LANES] = hi
                qx_sc[2 * p + e, :, LANES:2 * LANES] = lo

        def score_block(jb, carry):
            kh, kl = _split_bf16(kit_ref[0, jb])
            kx = jnp.concatenate([kh, kl, kh, kl], axis=0)
            acc = jnp.zeros((tq, tk), F32)
            for h in range(N_IDX_HEADS):
                rel = jnp.dot(qx_sc[h], kx, preferred_element_type=F32)
                acc = acc + w[:, h:h + 1] * jnp.maximum(rel, 0.0)
            adm, _, _ = admissible(jb)
            acc = jnp.where(adm, acc, NEG)
            bits = pltpu.bitcast(acc, jnp.int32)
            key_sc[jb] = bits ^ (jnp.right_shift(bits, 31) & 0x7FFFFFFF)
            return carry

        lax.fori_loop(0, nblk, score_block, 0)

        def bit_step(b, t_u):
            cand_u = t_u | lax.shift_left(jnp.int32(1), 31 - b)
            cand_s = cand_u ^ INT_MIN

            def count_block(jb, cnt):
                key = key_sc[jb]
                for c in range(tk // LANES):
                    cnt = cnt + jnp.where(key[:, c * LANES:(c + 1) * LANES] >= cand_s, 1.0, 0.0)
                return cnt

            cnt = lax.fori_loop(0, nblk, count_block, jnp.zeros((tq, LANES), F32))
            total = jnp.sum(cnt, axis=1, keepdims=True)
            return jnp.where(total >= float(topk), cand_u, t_u)

        t_u = lax.fori_loop(0, 32, bit_step, jnp.zeros((tq, 1), jnp.int32))
        thr_sc[...] = jnp.broadcast_to(t_u ^ INT_MIN, (tq, LANES))
        m_sc[...] = jnp.full(m_sc.shape, -jnp.inf, F32)
        l_sc[...] = jnp.zeros(l_sc.shape, F32)
        acc_sc[...] = jnp.zeros(acc_sc.shape, F32)

    @pl.when(j < nblk)
    def _attend():
        adm, qpos, kpos = admissible(j)
        sel = (key_sc[j] >= thr_sc[:, 0:1]) & adm
        dist = jnp.abs(qpos - kpos).astype(F32)
        for h in range(N_HEADS_A):
            hs = slice(h * HEAD_DIM, (h + 1) * HEAD_DIM)
            s = lax.dot_general(qa_ref[0, :, hs], k_ref[0, :, hs], (((1,), (1,)), ((), ())),
                                preferred_element_type=F32)
            s = s * ATTN_SCALE - (2.0 ** -(h + 1)) * dist
            s = jnp.where(sel, s, NEG)
            m_prev = m_sc[h]
            m_new = jnp.maximum(m_prev, jnp.max(s, axis=1, keepdims=True))
            alpha = jnp.exp(m_prev - m_new)
            p = jnp.exp(s - m_new)
            l_sc[h] = alpha * l_sc[h] + jnp.sum(p, axis=1, keepdims=True)
            acc_sc[:, hs] = alpha * acc_sc[:, hs] + jnp.dot(p.astype(BF16), v_ref[0, :, hs],
                                                           preferred_element_type=F32)
            m_sc[h] = m_new

    @pl.when(j == nkb - 1)
    def _finalize():
        for h in range(N_HEADS_A):
            hs = slice(h * HEAD_DIM, (h + 1) * HEAD_DIM)
            o_ref[0, :, hs] = (acc_sc[:, hs] / l_sc[h]).astype(o_ref.dtype)


def _mixer_a(q2, qi, wi, ki_all, k_all, v_all, *, past, valid_len, topk, tq, tk):
    b, t, _ = q2.shape
    lp = k_all.shape[1]
    nkb = lp // tk
    assert lp % tk == 0 and t % tq == 0 and tk % LANES == 0 and past % CHUNK == 0
    kit = jnp.swapaxes(jnp.swapaxes(ki_all, 1, 2).reshape(b, IDX_DIM, nkb, tk), 1, 2)

    def last_block(i):
        kend = jnp.minimum(((past + (i + 1) * tq + CHUNK - 1) // CHUNK) * CHUNK, valid_len)
        return (kend + tk - 1) // tk - 1

    kv_map = lambda bb, i, j: (bb, jnp.minimum(j, last_block(i)), 0)
    kernel = functools.partial(_mixer_a_kernel, tq=tq, tk=tk, nkb=nkb, past=past,
                               valid_len=valid_len, topk=topk)
    return pl.pallas_call(
        kernel,
        grid=(b, t // tq, nkb),
        in_specs=[pl.BlockSpec((1, tq, W_A), lambda bb, i, j: (bb, i, 0)),
                  pl.BlockSpec((1, tq, N_IDX_HEADS * IDX_DIM), lambda bb, i, j: (bb, i, 0)),
                  pl.BlockSpec((1, tq, N_IDX_HEADS), lambda bb, i, j: (bb, i, 0)),
                  pl.BlockSpec((1, nkb, IDX_DIM, tk), lambda bb, i, j: (bb, 0, 0, 0)),
                  pl.BlockSpec((1, tk, W_A), kv_map),
                  pl.BlockSpec((1, tk, W_A), kv_map)],
        out_specs=pl.BlockSpec((1, tq, W_A), lambda bb, i, j: (bb, i, 0)),
        out_shape=jax.ShapeDtypeStruct((b, t, W_A), BF16),
        scratch_shapes=[pltpu.VMEM((nkb, tq, tk), jnp.int32),
                        pltpu.VMEM((tq, LANES), jnp.int32),
                        pltpu.VMEM((N_HEADS_A, tq, 1), F32),
                        pltpu.VMEM((N_HEADS_A, tq, 1), F32),
                        pltpu.VMEM((tq, W_A), F32),
                        pltpu.VMEM((N_IDX_HEADS, tq, 2 * LANES), BF16)],
        compiler_params=_cp("arbitrary", "arbitrary", "arbitrary"),
        name="mixer_a",
    )(q2, qi, wi, kit, k_all, v_all)


def _mixer_b_kernel(*refs, nkb, tkb):
    q_ref = refs[0]
    k_refs = refs[1:1 + nkb]
    v_refs = refs[1 + nkb:1 + 2 * nkb]
    bias_ref = refs[1 + 2 * nkb]
    o_ref = refs[2 + 2 * nkb]
    for h in range(N_HEADS_B):
        hs = slice(h * HEAD_DIM, (h + 1) * HEAD_DIM)
        q = q_ref[0, :, hs]
        parts = [lax.dot_general(q, kr[0, :, hs], (((1,), (1,)), ((), ())), preferred_element_type=F32)
                 for kr in k_refs]
        s = parts[0] if nkb == 1 else jnp.concatenate(parts, axis=1)
        s = s * ATTN_SCALE + bias_ref[0, h]
        p = jnp.exp(s - jnp.max(s, axis=1, keepdims=True))
        l = jnp.sum(p, axis=1, keepdims=True)
        pb = p.astype(BF16)
        pv = None
        for c, vr in enumerate(v_refs):
            t = jnp.dot(pb[:, c * tkb:(c + 1) * tkb], vr[0, :, hs], preferred_element_type=F32)
            pv = t if pv is None else pv + t
        o_ref[0, :, hs] = (pv / l).astype(o_ref.dtype)


def _band_bias(rel_bias, tq, wk, off, lows, hi):
    r = jnp.arange(tq, dtype=jnp.int32)[:, None]
    c = jnp.arange(wk, dtype=jnp.int32)[None, :]
    d = r + off - c
    bias = rel_bias[:, jnp.clip(d, -REL_CLIP, REL_CLIP) + REL_CLIP].astype(F32)
    dq = r // CHUNK
    dk = jnp.floor_divide(c - off, CHUNK)
    ok = (dk <= dq) & (dk >= dq - BAND_CHUNKS) & (c < hi)
    tiles = [jnp.where((ok & (c >= lo))[None], bias, NEG) for lo in lows]
    return jnp.stack(tiles)


def _mixer_b(q2, kv, bias, *, tq, tkb, nkb, back):
    b, t, _ = q2.shape
    nvar = bias.shape[0]
    wk = nkb * tkb
    kmaps = [functools.partial(lambda bb, i, p, col: (bb, jnp.maximum(i - back + p, 0), col), p=p, col=0)
             for p in range(nkb)]
    vmaps = [functools.partial(lambda bb, i, p, col: (bb, jnp.maximum(i - back + p, 0), col), p=p, col=1)
             for p in range(nkb)]
    kernel = functools.partial(_mixer_b_kernel, nkb=nkb, tkb=tkb)
    return pl.pallas_call(
        kernel,
        grid=(b, t // tq),
        in_specs=([pl.BlockSpec((1, tq, W_B), lambda bb, i: (bb, i, 1))]
                  + [pl.BlockSpec((1, tkb, W_B), m) for m in kmaps]
                  + [pl.BlockSpec((1, tkb, W_B), m) for m in vmaps]
                  + [pl.BlockSpec((1, N_HEADS_B, tq, wk), lambda bb, i: (jnp.minimum(i, nvar - 1), 0, 0, 0))]),
        out_specs=pl.BlockSpec((1, tq, W_B), lambda bb, i: (bb, i, 0)),
        out_shape=jax.ShapeDtypeStruct((b, t, W_B), BF16),
        compiler_params=_cp("arbitrary", "arbitrary"),
        name="mixer_b",
    )(q2, *([kv] * (2 * nkb)), bias)


def _merge_kernel(h_ref, oa_ref, ob_ref, pa_ref, pb_ref, wga_ref, wgb_ref, bga_ref, bgb_ref, o_ref):
    h = h_ref[...]
    ga = jax.nn.sigmoid(jnp.dot(h, wga_ref[...], preferred_element_type=F32) + bga_ref[...])
    gb = jax.nn.sigmoid(jnp.dot(h, wgb_ref[...], preferred_element_type=F32) + bgb_ref[...])
    ya = jnp.dot(oa_ref[...], pa_ref[...], preferred_element_type=F32)
    yb = jnp.dot(ob_ref[...], pb_ref[...], preferred_element_type=F32)
    o_ref[...] = (ga * ya + gb * yb).astype(o_ref.dtype)


def _merge(h, oa, ob, pa, pb, wg, bg):
    m, d = h.shape
    tm = _pick(m, 512, 16)
    tn = _pick(d, 512, LANES)
    nb = d // tn
    bg2 = bg.reshape(1, 2 * d)
    return pl.pallas_call(
        _merge_kernel,
        grid=(m // tm, nb),
        in_specs=[pl.BlockSpec((tm, d), lambda i, j: (i, 0)),
                  pl.BlockSpec((tm, W_A), lambda i, j: (i, 0)),
                  pl.BlockSpec((tm, W_B), lambda i, j: (i, 0)),
                  pl.BlockSpec((W_A, tn), lambda i, j: (0, j)),
                  pl.BlockSpec((W_B, tn), lambda i, j: (0, j)),
                  pl.BlockSpec((d, tn), lambda i, j: (0, j)),
                  pl.BlockSpec((d, tn), lambda i, j: (0, j + nb)),
                  pl.BlockSpec((1, tn), lambda i, j: (0, j)),
                  pl.BlockSpec((1, tn), lambda i, j: (0, j + nb))],
        out_specs=pl.BlockSpec((tm, tn), lambda i, j: (i, j)),
        out_shape=jax.ShapeDtypeStruct((m, d), BF16),
        compiler_params=_cp("arbitrary", "arbitrary"),
        name="merge",
    )(h, oa, ob, pa, pb, wg, wg, bg2, bg2)


def _outproj_kernel(mg_ref, x_ref, w_ref, mod_ref, g_ref, x1_ref, h2_ref):
    o = jnp.dot(mg_ref[0], w_ref[...], preferred_element_type=F32)
    x1 = x_ref[0] + mod_ref[0, 2:3, :] * o
    x1_ref[0] = x1
    h2_ref[0] = _rms_mod(x1, g_ref[...], mod_ref[0, 3:4, :], mod_ref[0, 4:5, :]).astype(h2_ref.dtype)


def _outproj(merged, x, w_out, mod, g2):
    b, t, d = x.shape
    tm = _pick(t, 256, 16)
    row = lambda i, j: (i, j, 0)
    return pl.pallas_call(
        _outproj_kernel,
        grid=(b, t // tm),
        in_specs=[pl.BlockSpec((1, tm, d), row),
                  pl.BlockSpec((1, tm, d), row),
                  pl.BlockSpec((d, d), lambda i, j: (0, 0)),
                  pl.BlockSpec((1, 6, d), lambda i, j: (i, 0, 0)),
                  pl.BlockSpec((1, d), lambda i, j: (0, 0))],
        out_specs=[pl.BlockSpec((1, tm, d), row), pl.BlockSpec((1, tm, d), row)],
        out_shape=[jax.ShapeDtypeStruct((b, t, d), F32), jax.ShapeDtypeStruct((b, t, d), BF16)],
        compiler_params=_cp("arbitrary", "arbitrary"),
        name="outproj",
    )(merged, x, w_out, mod, g2.reshape(1, d))


def _ffn_kernel(h2_ref, x1_ref, mod_ref, wa_ref, wb_ref, wd_ref, wc_ref, bc_ref, prev_ref, fg_ref,
                y_ref, st_ref, abuf, carry, acc_sc, *, tm, nf):
    m = pl.program_id(1)
    f = pl.program_id(2)
    h2 = h2_ref[0]
    a = jnp.dot(h2, wa_ref[...], preferred_element_type=F32)
    gate = jnp.dot(h2, wb_ref[...], preferred_element_type=F32)

    @pl.when(m == 0)
    def _():
        abuf[6:8, :] = prev_ref[0]

    @pl.when(m > 0)
    def _():
        abuf[6:8, :] = carry[f]

    abuf[8:8 + tm, :] = a
    tail = a[tm - 2:tm, :]
    carry[f] = tail
    st_ref[0, 0] = tail
    conv = (bc_ref[...] + wc_ref[0:1, :] * abuf[6:6 + tm, :] + wc_ref[1:2, :] * abuf[7:7 + tm, :]
            + wc_ref[2:3, :] * a)
    u = (conv * jax.nn.sigmoid(conv) * gate).astype(BF16)
    part = jnp.dot(u, wd_ref[...], preferred_element_type=F32)

    @pl.when(f == 0)
    def _():
        acc_sc[...] = part

    @pl.when(f > 0)
    def _():
        acc_sc[...] += part

    @pl.when(f == nf - 1)
    def _():
        x2 = x1_ref[0] + mod_ref[0, 5:6, :] * acc_sc[...]
        y_ref[0] = x2 * lax.rsqrt(jnp.mean(x2 * x2, axis=-1, keepdims=True) + EPS) * fg_ref[...]


def _ffn(h2, x1, mod, wa, wb, wd, wconv, bconv, prev, final_g):
    b, t, d = x1.shape
    dff = wa.shape[1]
    tm = _pick(t, 512, 16)
    tf = _pick(dff, 512, LANES)
    nf = dff // tf
    row = lambda i, j, k: (i, j, 0)
    kernel = functools.partial(_ffn_kernel, tm=tm, nf=nf)
    return pl.pallas_call(
        kernel,
        grid=(b, t // tm, nf),
        in_specs=[pl.BlockSpec((1, tm, d), row),
                  pl.BlockSpec((1, tm, d), row),
                  pl.BlockSpec((1, 6, d), lambda i, j, k: (i, 0, 0)),
                  pl.BlockSpec((d, tf), lambda i, j, k: (0, k)),
                  pl.BlockSpec((d, tf), lambda i, j, k: (0, k)),
                  pl.BlockSpec((tf, d), lambda i, j, k: (k, 0)),
                  pl.BlockSpec((CONV_W, tf), lambda i, j, k: (0, k)),
                  pl.BlockSpec((1, tf), lambda i, j, k: (0, k)),
                  pl.BlockSpec((1, CONV_W - 1, tf), lambda i, j, k: (i, 0, k)),
                  pl.BlockSpec((1, d), lambda i, j, k: (0, 0))],
        out_specs=[pl.BlockSpec((1, tm, d), row),
                   pl.BlockSpec((1, 1, CONV_W - 1, tf), lambda i, j, k: (i, j, 0, k))],
        out_shape=[jax.ShapeDtypeStruct((b, t, d), F32),
                   jax.ShapeDtypeStruct((b, t // tm, CONV_W - 1, dff), F32)],
        scratch_shapes=[pltpu.VMEM((tm + 8, tf), F32),
                        pltpu.VMEM((nf, CONV_W - 1, tf), F32),
                        pltpu.VMEM((tm, d), F32)],
        compiler_params=_cp("arbitrary", "arbitrary", "arbitrary"),
        name="ffn",
    )(h2, x1, mod, wa, wb, wd, wconv, bconv.reshape(1, dff), prev, final_g.reshape(1, d))


def _prep_weights(w_in, w_gate, w_proj_a, w_proj_b, w_out, w_up, w_down):
    o = np.cumsum((0, W_A, W_A, W_A, N_IDX_HEADS * IDX_DIM, IDX_DIM, N_IDX_HEADS, W_B, W_B, W_B))
    col = lambda a, b: w_in[:, int(o[a]):int(o[b])]
    d = w_in.shape[0]
    dff = w_down.shape[0]
    pad = jnp.zeros((d, LANES - IDX_DIM - N_IDX_HEADS), w_in.dtype)
    qi_hi, qi_lo = _split_bf16(col(3, 4))
    kiwi_hi, kiwi_lo = _split_bf16(jnp.concatenate([col(4, 6), pad], axis=1))
    return dict(
        q2=jnp.concatenate([col(0, 1), col(6, 7)], axis=1).astype(BF16),
        ka=col(1, 2).astype(BF16),
        va=col(2, 3).astype(BF16),
        qi_hi=qi_hi, qi_lo=qi_lo, kiwi_hi=kiwi_hi, kiwi_lo=kiwi_lo,
        kvb=col(7, 9).astype(BF16),
        gate=w_gate.astype(BF16),
        pa=w_proj_a.astype(BF16),
        pb=w_proj_b.astype(BF16),
        out=w_out.astype(BF16),
        up_a=w_up[:, :dff].astype(BF16),
        up_b=w_up[:, dff:].astype(BF16),
        down=w_down.astype(BF16),
    )


def _trunk_layer(x, mod, cache, wts, norm1_g, rel_bias, b_gate, norm2_g, w_conv, b_conv, final_g):
    b, t, d = x.shape
    dff = wts["down"].shape[0]
    h, h_lo = _norm_mod(x, norm1_g, mod)
    h2d = h.reshape(b * t, d)
    hl2d = h_lo.reshape(b * t, d)
    (q2,) = _mm(h2d, wts["q2"], (BF16,))
    ka32, ka16 = _mm(h2d, wts["ka"], (F32, BF16))
    va32, va16 = _mm(h2d, wts["va"], (F32, BF16))
    qi = _mm_split(h2d, hl2d, wts["qi_hi"], wts["qi_lo"]).reshape(b, t, N_IDX_HEADS * IDX_DIM)
    kiwi = _mm_split(h2d, hl2d, wts["kiwi_hi"], wts["kiwi_lo"])
    (kvb16,) = _mm(h2d, wts["kvb"], (BF16,))
    q2 = q2.reshape(b, t, W_A + W_B)
    kiwi = kiwi.reshape(b, t, LANES)
    ki = kiwi[:, :, :IDX_DIM]
    wi = kiwi[:, :, IDX_DIM:IDX_DIM + N_IDX_HEADS]
    ka16 = ka16.reshape(b, t, W_A)
    va16 = va16.reshape(b, t, W_A)
    kvb16 = kvb16.reshape(b, t, 2 * W_B)

    if cache is None:
        rows_tail = min(BAND_PAST, t)
        tq_a = _pick(t, 256, CHUNK)
        tk_a = _pick(t, 512, LANES)
        o_a = _mixer_a(q2, qi, wi, ki, ka16, va16, past=0, valid_len=t, topk=min(TOPK_MAX, t // 4),
                       tq=tq_a, tk=tk_a)
        tq_b = _pick(t, 256, CHUNK)
        assert BAND_PAST % tq_b == 0
        back = BAND_PAST // tq_b
        lows = [BAND_PAST - v * tq_b for v in range(back + 1)]
        bias = _band_bias(rel_bias, tq_b, BAND_PAST + tq_b, BAND_PAST, lows, BAND_PAST + tq_b)
        o_b = _mixer_b(q2, kvb16, bias, tq=tq_b, tkb=tq_b, nkb=back + 1, back=back)
        conv_prev = jnp.zeros((b, CONV_W - 1, dff), F32)
    else:
        ck, cv, cki, cbk, cbv, conv_prev = cache
        past = ck.shape[1]
        rows_tail = t
        tk_a = LANES
        l_valid = past + t
        lp = -(-l_valid // tk_a) * tk_a
        padk = jnp.zeros((b, lp - l_valid, W_A), BF16)
        k_all = jnp.concatenate([ck.reshape(b, past, W_A).astype(BF16), ka16, padk], axis=1)
        v_all = jnp.concatenate([cv.reshape(b, past, W_A).astype(BF16), va16, padk], axis=1)
        ki_all = jnp.concatenate([cki, ki, jnp.zeros((b, lp - l_valid, IDX_DIM), F32)], axis=1)
        o_a = _mixer_a(q2, qi, wi, ki_all, k_all, v_all, past=past, valid_len=l_valid,
                       topk=min(TOPK_MAX, l_valid // 4), tq=t, tk=tk_a)
        rows = cbk.shape[1]
        assert past % CHUNK == 0 and rows % CHUNK == 0
        lb_valid = rows + t
        lb = -(-lb_valid // LANES) * LANES
        kv_cache = jnp.concatenate([cbk.reshape(b, rows, W_B), cbv.reshape(b, rows, W_B)], axis=2).astype(BF16)
        kv_all = jnp.concatenate([kv_cache, kvb16, jnp.zeros((b, lb - lb_valid, 2 * W_B), BF16)], axis=1)
        bias = _band_bias(rel_bias, t, lb, rows, [0], lb_valid)
        o_b = _mixer_b(q2, kv_all, bias, tq=t, tkb=lb, nkb=1, back=0)

    (tail32,) = _mm(h[:, t - rows_tail:].reshape(b * rows_tail, d), wts["kvb"], (F32,))
    tail32 = tail32.reshape(b, rows_tail, 2, N_HEADS_B, HEAD_DIM)
    new_bk, new_bv = tail32[:, :, 0], tail32[:, :, 1]

    merged = _merge(h2d, o_a.reshape(b * t, W_A), o_b.reshape(b * t, W_B), wts["pa"], wts["pb"],
                    wts["gate"], b_gate)
    x1, h2 = _outproj(merged.reshape(b, t, d), x, wts["out"], mod, norm2_g)
    y, tails = _ffn(h2, x1, mod, wts["up_a"], wts["up_b"], wts["down"], w_conv, b_conv,
                    conv_prev, final_g)
    conv_state = tails[:, -1]
    state = (ka32.reshape(b, t, N_HEADS_A, HEAD_DIM), va32.reshape(b, t, N_HEADS_A, HEAD_DIM), ki,
             new_bk, new_bv, conv_state)
    return y, state


def kernel(x_prompt, x_sample, cache_a_k, cache_a_v, cache_idx_k, cache_b_k, cache_b_v, state_ffn_conv,
           c_prompt, c_sample, w_ada, b_ada, norm1_g, w_in, rel_bias, w_gate, b_gate, w_proj_a, w_proj_b,
           w_out, norm2_g, w_up, w_conv, b_conv, w_down, final_g):
    depth = w_ada.shape[0]
    assert depth == 1, "the fused final RMSNorm assumes a single layer"
    d = x_prompt.shape[-1]
    nb_p = x_prompt.shape[0]
    c_all = jnp.concatenate([c_prompt, c_sample], axis=0)
    xp, xs = x_prompt, x_sample
    states_p, states_s = [], []
    for l in range(depth):
        wts = _prep_weights(w_in[l], w_gate[l], w_proj_a[l], w_proj_b[l], w_out[l], w_up[l], w_down[l])
        mod = _ada(c_all, w_ada[l], b_ada[l]).reshape(c_all.shape[0], 6, d)
        args = (wts, norm1_g[l], rel_bias[l], b_gate[l], norm2_g[l], w_conv[l], b_conv[l], final_g)
        xp, st_p = _trunk_layer(xp, mod[:nb_p], None, *args)
        states_p.append(st_p)
        cache_l = (cache_a_k[l], cache_a_v[l], cache_idx_k[l], cache_b_k[l], cache_b_v[l], state_ffn_conv[l])
        xs, st_s = _trunk_layer(xs, mod[nb_p:], cache_l, *args)
        states_s.append(st_s)
    sp = [t[0][None] for t in zip(*states_p)]
    ss = [t[0][None] for t in zip(*states_s)]
    return (xp, xs, *sp, *ss)
```

```python
import functools

import jax
import jax.numpy as jnp
import numpy as np
from jax import lax
from jax.experimental import pallas as pl
from jax.experimental.pallas import tpu as pltpu

F32 = jnp.float32
BF16 = jnp.bfloat16

CHUNK = 64
CHUNK_SHIFT = 6
HEAD_DIM = 128
N_HEADS_A = 8
N_IDX_HEADS = 16
IDX_DIM = 64
TOPK_MAX = 256
N_HEADS_B = 8
BAND_CHUNKS = 8
BAND_PAST = BAND_CHUNKS * CHUNK
REL_CLIP = 2 * CHUNK
CONV_W = 3
EPS = 1e-6
NEG = -1e30
W_A = N_HEADS_A * HEAD_DIM
W_B = N_HEADS_B * HEAD_DIM
ATTN_SCALE = HEAD_DIM ** -0.5
LOG2E = 1.4426950408889634
INT_MIN = -(2 ** 31)
LANES = 128
MXU_COLS = 256
VMEM_LIMIT = 56 * 1024 * 1024


def _cp(*sem):
    return pltpu.CompilerParams(dimension_semantics=sem, vmem_limit_bytes=VMEM_LIMIT)


def _pick(dim, pref, align):
    t = min(pref, dim)
    t -= t % align
    while t >= align:
        if dim % t == 0:
            return t
        t -= align
    return dim


def _ada_kernel(c_ref, w_ref, b_ref, o_ref):
    c = c_ref[...]
    s = c * jax.nn.sigmoid(c)
    o_ref[...] = jnp.dot(s, w_ref[...], preferred_element_type=F32,
                         precision=lax.Precision.HIGHEST) + b_ref[...]


def _ada(c, w, b):
    r, d = c.shape
    n = w.shape[1]
    tn = _pick(n, 1024, LANES)
    return pl.pallas_call(
        _ada_kernel,
        grid=(n // tn,),
        in_specs=[pl.BlockSpec((r, d), lambda j: (0, 0)),
                  pl.BlockSpec((d, tn), lambda j: (0, j)),
                  pl.BlockSpec((1, tn), lambda j: (0, j))],
        out_specs=pl.BlockSpec((r, tn), lambda j: (0, j)),
        out_shape=jax.ShapeDtypeStruct((r, n), F32),
        compiler_params=_cp("arbitrary"),
        name="ada",
    )(c, w, b.reshape(1, n))


def _rms_mod(x, g, shift, scale):
    y = x * lax.rsqrt(jnp.mean(x * x, axis=-1, keepdims=True) + EPS) * g
    return y * (1.0 + scale) + shift


def _split_bf16(x):
    hi = x.astype(BF16)
    return hi, (x - hi.astype(F32)).astype(BF16)


def _norm_mod_kernel(x_ref, g_ref, mod_ref, hi_ref, lo_ref):
    h = _rms_mod(x_ref[0], g_ref[...], mod_ref[0, 0:1, :], mod_ref[0, 1:2, :])
    hi, lo = _split_bf16(h)
    hi_ref[0] = hi
    lo_ref[0] = lo


def _norm_mod(x, g, mod):
    b, t, d = x.shape
    tm = _pick(t, 512, 16)
    row = lambda i, j: (i, j, 0)
    return pl.pallas_call(
        _norm_mod_kernel,
        grid=(b, t // tm),
        in_specs=[pl.BlockSpec((1, tm, d), row),
                  pl.BlockSpec((1, d), lambda i, j: (0, 0)),
                  pl.BlockSpec((1, 6, d), lambda i, j: (i, 0, 0))],
        out_specs=[pl.BlockSpec((1, tm, d), row), pl.BlockSpec((1, tm, d), row)],
        out_shape=[jax.ShapeDtypeStruct((b, t, d), BF16), jax.ShapeDtypeStruct((b, t, d), BF16)],
        compiler_params=_cp("arbitrary", "arbitrary"),
        name="norm_mod",
    )(x, g.reshape(1, d), mod)


def _mm_kernel(a_ref, w_ref, *o_refs):
    acc = jnp.dot(a_ref[...], w_ref[...], preferred_element_type=F32)
    for o_ref in o_refs:
        o_ref[...] = acc.astype(o_ref.dtype)


def _mm(a, w, dtypes):
    m, k = a.shape
    n = w.shape[1]
    tm = _pick(m, 1024, 16)
    tn = _pick(n, 1024, LANES)
    outs = pl.pallas_call(
        _mm_kernel,
        grid=(m // tm, n // tn),
        in_specs=[pl.BlockSpec((tm, k), lambda i, j: (i, 0)),
                  pl.BlockSpec((k, tn), lambda i, j: (0, j))],
        out_specs=[pl.BlockSpec((tm, tn), lambda i, j: (i, j)) for _ in dtypes],
        out_shape=[jax.ShapeDtypeStruct((m, n), dt) for dt in dtypes],
        compiler_params=_cp("arbitrary", "arbitrary"),
        name="proj",
    )(a, w)
    return outs


def _mm_split_kernel(ah_ref, al_ref, wh_ref, wl_ref, o_ref):
    ah = ah_ref[...]
    acc = jnp.dot(ah, wh_ref[...], preferred_element_type=F32)
    acc = acc + jnp.dot(al_ref[...], wh_ref[...], preferred_element_type=F32)
    acc = acc + jnp.dot(ah, wl_ref[...], preferred_element_type=F32)
    o_ref[...] = acc


def _mm_split(a_hi, a_lo, w_hi, w_lo):
    m, k = a_hi.shape
    n = w_hi.shape[1]
    tm = _pick(m, 1024, 16)
    tn = _pick(n, 1024, LANES)
    a_spec = pl.BlockSpec((tm, k), lambda i, j: (i, 0))
    w_spec = pl.BlockSpec((k, tn), lambda i, j: (0, j))
    return pl.pallas_call(
        _mm_split_kernel,
        grid=(m // tm, n // tn),
        in_specs=[a_spec, a_spec, w_spec, w_spec],
        out_specs=pl.BlockSpec((tm, tn), lambda i, j: (i, j)),
        out_shape=jax.ShapeDtypeStruct((m, n), F32),
        compiler_params=_cp("arbitrary", "arbitrary"),
        name="proj_split",
    )(a_hi, a_lo, w_hi, w_lo)


def _mixer_a_kernel(qa_ref, qi_ref, wit_ref, ki2_ref, k_ref, vt_ref, o_ref,
                    key_sc, thr_sc, m_sc, l_sc, acc_sc, qx_sc,
                    *, tq, tk, nkb, past, valid_len, topk):
    i = pl.program_id(1)
    j = pl.program_id(2)
    q0 = past + i * tq
    kend = jnp.minimum(((q0 + tq + CHUNK - 1) // CHUNK) * CHUNK, valid_len)
    nblk = (kend + tk - 1) // tk
    slab = 8

    def admissible(jb):
        kpos = jb * tk + lax.broadcasted_iota(jnp.int32, (tk, tq), 0)
        qpos = q0 + lax.broadcasted_iota(jnp.int32, (tk, tq), 1)
        adm = (jnp.right_shift(kpos, CHUNK_SHIFT) <= jnp.right_shift(qpos, CHUNK_SHIFT)) & (kpos < valid_len)
        return adm, qpos, kpos

    @pl.when(j == 0)
    def _index_and_threshold():
        qt = qi_ref[0].T
        q_hi, q_lo = _split_bf16(qt)
        for h in range(N_IDX_HEADS):
            rows = slice(h * IDX_DIM, (h + 1) * IDX_DIM)
            qx_sc[h, 0 * IDX_DIM:1 * IDX_DIM, :] = q_hi[rows]
            qx_sc[h, 1 * IDX_DIM:2 * IDX_DIM, :] = q_hi[rows]
            qx_sc[h, 2 * IDX_DIM:3 * IDX_DIM, :] = q_lo[rows]
            qx_sc[h, 3 * IDX_DIM:4 * IDX_DIM, :] = q_lo[rows]

        first = lax.broadcasted_iota(jnp.int32, (tk, LANES), 1) < IDX_DIM

        def score_block(jb, carry):
            dup = ki2_ref[0, jb]
            kk = jnp.where(first, dup, dup - dup.astype(BF16).astype(F32)).astype(BF16)
            kx = jnp.concatenate([kk, kk], axis=1)
            acc = jnp.zeros((tk, tq), F32)
            for h in range(N_IDX_HEADS):
                rel = jnp.dot(kx, qx_sc[h], preferred_element_type=F32)
                acc = acc + wit_ref[0, h:h + 1, :] * jnp.maximum(rel, 0.0)
            adm, _, _ = admissible(jb)
            acc = jnp.where(adm, acc, NEG)
            bits = pltpu.bitcast(acc, jnp.int32)
            key_sc[jb] = bits ^ (jnp.right_shift(bits, 31) & 0x7FFFFFFF)
            return carry

        lax.fori_loop(0, nblk, score_block, 0)

        def bit_step(b, t_u):
            cand_u = t_u | lax.shift_left(jnp.int32(1), 31 - b)
            cand_s = cand_u ^ INT_MIN

            def count_block(jb, cnt):
                for r in range(tk // slab):
                    cnt = cnt + jnp.where(key_sc[jb, r * slab:(r + 1) * slab, :] >= cand_s, 1.0, 0.0)
                return cnt

            cnt = lax.fori_loop(0, nblk, count_block, jnp.zeros((slab, tq), F32))
            total = jnp.sum(cnt, axis=0, keepdims=True)
            return jnp.where(total >= float(topk), cand_u, t_u)

        t_u = lax.fori_loop(0, 32, bit_step, jnp.zeros((slab, tq), jnp.int32))
        thr_sc[...] = t_u ^ INT_MIN
        m_sc[...] = jnp.full(m_sc.shape, -jnp.inf, F32)
        l_sc[...] = jnp.zeros(l_sc.shape, F32)
        acc_sc[...] = jnp.zeros(acc_sc.shape, F32)

    @pl.when(j < nblk)
    def _attend():
        adm, qpos, kpos = admissible(j)
        sel = (key_sc[j] >= thr_sc[0:1, :]) & adm
        dist = jnp.where(sel, jnp.abs(qpos - kpos).astype(F32), -NEG)

        def logits(h):
            hs = slice(h * HEAD_DIM, (h + 1) * HEAD_DIM)
            return lax.dot_general(k_ref[0, :, hs], qa_ref[0, :, hs], (((1,), (1,)), ((), ())),
                                   preferred_element_type=F32)

        s_next = logits(0)
        for h in range(N_HEADS_A):
            hs = slice(h * HEAD_DIM, (h + 1) * HEAD_DIM)
            s_raw = s_next
            if h + 1 < N_HEADS_A:
                s_next = logits(h + 1)
            s = s_raw * (ATTN_SCALE * LOG2E) - ((2.0 ** -(h + 1)) * LOG2E) * dist
            m_prev = m_sc[h:h + 1, :]
            m_new = jnp.maximum(m_prev, jnp.max(s, axis=0, keepdims=True))
            alpha = jnp.exp2(m_prev - m_new)
            p = jnp.exp2(s - m_new)
            l_sc[h:h + 1, :] = alpha * l_sc[h:h + 1, :] + jnp.sum(p, axis=0, keepdims=True)
            acc_sc[hs, :] = alpha * acc_sc[hs, :] + jnp.dot(vt_ref[0, hs, :], p.astype(BF16),
                                                           preferred_element_type=F32)
            m_sc[h:h + 1, :] = m_new

    @pl.when(j == nkb - 1)
    def _finalize():
        for h in range(N_HEADS_A):
            hs = slice(h * HEAD_DIM, (h + 1) * HEAD_DIM)
            o_ref[0, :, hs] = (acc_sc[hs, :] / l_sc[h:h + 1, :]).T.astype(o_ref.dtype)


def _mixer_a(q2, qi, wi, ki_all, k_all, v_all, *, past, valid_len, topk, tq, tk):
    b, t, _ = q2.shape
    lp = k_all.shape[1]
    nkb = lp // tk
    assert lp % tk == 0 and t % tq == 0 and tk % LANES == 0 and past % CHUNK == 0
    ki2 = jnp.concatenate([ki_all, ki_all], axis=2).reshape(b, nkb, tk, 2 * IDX_DIM)
    wit = jnp.swapaxes(wi, 1, 2)
    vt = jnp.swapaxes(v_all, 1, 2)

    def last_block(i):
        kend = jnp.minimum(((past + (i + 1) * tq + CHUNK - 1) // CHUNK) * CHUNK, valid_len)
        return (kend + tk - 1) // tk - 1

    k_map = lambda bb, i, j: (bb, jnp.minimum(j, last_block(i)), 0)
    vt_map = lambda bb, i, j: (bb, 0, jnp.minimum(j, last_block(i)))
    kernel = functools.partial(_mixer_a_kernel, tq=tq, tk=tk, nkb=nkb, past=past,
                               valid_len=valid_len, topk=topk)
    return pl.pallas_call(
        kernel,
        grid=(b, t // tq, nkb),
        in_specs=[pl.BlockSpec((1, tq, W_A), lambda bb, i, j: (bb, i, 0)),
                  pl.BlockSpec((1, tq, N_IDX_HEADS * IDX_DIM), lambda bb, i, j: (bb, i, 0)),
                  pl.BlockSpec((1, N_IDX_HEADS, tq), lambda bb, i, j: (bb, 0, i)),
                  pl.BlockSpec((1, nkb, tk, 2 * IDX_DIM), lambda bb, i, j: (bb, 0, 0, 0)),
                  pl.BlockSpec((1, tk, W_A), k_map),
                  pl.BlockSpec((1, W_A, tk), vt_map)],
        out_specs=pl.BlockSpec((1, tq, W_A), lambda bb, i, j: (bb, i, 0)),
        out_shape=jax.ShapeDtypeStruct((b, t, W_A), BF16),
        scratch_shapes=[pltpu.VMEM((nkb, tk, tq), jnp.int32),
                        pltpu.VMEM((8, tq), jnp.int32),
                        pltpu.VMEM((N_HEADS_A, tq), F32),
                        pltpu.VMEM((N_HEADS_A, tq), F32),
                        pltpu.VMEM((W_A, tq), F32),
                        pltpu.VMEM((N_IDX_HEADS, 4 * IDX_DIM, tq), BF16)],
        compiler_params=_cp("arbitrary", "arbitrary", "arbitrary"),
        name="mixer_a",
    )(q2, qi, wit, ki2, k_all, vt)


def _mixer_b_kernel(*refs, nkb, tkb):
    q_ref = refs[0]
    k_refs = refs[1:1 + nkb]
    v_refs = refs[1 + nkb:1 + 2 * nkb]
    bias_ref = refs[1 + 2 * nkb]
    o_ref = refs[2 + 2 * nkb]
    for h in range(N_HEADS_B):
        hs = slice(h * HEAD_DIM, (h + 1) * HEAD_DIM)
        q = q_ref[0, :, hs]
        parts = [lax.dot_general(q, kr[0, :, hs], (((1,), (1,)), ((), ())), preferred_element_type=F32)
                 for kr in k_refs]
        s = parts[0] if nkb == 1 else jnp.concatenate(parts, axis=1)
        s = s * ATTN_SCALE + bias_ref[0, h]
        p = jnp.exp(s - jnp.max(s, axis=1, keepdims=True))
        l = jnp.sum(p, axis=1, keepdims=True)
        pb = p.astype(BF16)
        pv = None
        for c, vr in enumerate(v_refs):
            t = jnp.dot(pb[:, c * tkb:(c + 1) * tkb], vr[0, :, hs], preferred_element_type=F32)
            pv = t if pv is None else pv + t
        o_ref[0, :, hs] = (pv / l).astype(o_ref.dtype)


def _band_bias(rel_bias, tq, wk, off, lows, hi):
    nh = rel_bias.shape[0]
    n = wk + tq
    e = np.concatenate([np.arange(wk + 1), np.arange(-(tq - 1), 0)])
    f = rel_bias[:, np.clip(off - e, -REL_CLIP, REL_CLIP) + REL_CLIP].astype(F32)
    bias = jnp.tile(f, (1, tq))[:, :tq * (n - 1)].reshape(nh, tq, n - 1)[:, :, :wk]
    r = np.arange(tq)[:, None]
    c = np.arange(wk)[None, :]
    dq = r // CHUNK
    dk = np.floor_divide(c - off, CHUNK)
    ok = (dk <= dq) & (dk >= dq - BAND_CHUNKS) & (c < hi)
    tiles = [jnp.where((ok & (c >= lo))[None], bias, NEG) for lo in lows]
    return jnp.stack(tiles)


def _mixer_b(q2, kv, bias, *, tq, tkb, nkb, back):
    b, t, _ = q2.shape
    nvar = bias.shape[0]
    wk = nkb * tkb
    kmaps = [functools.partial(lambda bb, i, p, col: (bb, jnp.maximum(i - back + p, 0), col), p=p, col=0)
             for p in range(nkb)]
    vmaps = [functools.partial(lambda bb, i, p, col: (bb, jnp.maximum(i - back + p, 0), col), p=p, col=1)
             for p in range(nkb)]
    kernel = functools.partial(_mixer_b_kernel, nkb=nkb, tkb=tkb)
    return pl.pallas_call(
        kernel,
        grid=(b, t // tq),
        in_specs=([pl.BlockSpec((1, tq, W_B), lambda bb, i: (bb, i, 1))]
                  + [pl.BlockSpec((1, tkb, W_B), m) for m in kmaps]
                  + [pl.BlockSpec((1, tkb, W_B), m) for m in vmaps]
                  + [pl.BlockSpec((1, N_HEADS_B, tq, wk), lambda bb, i: (jnp.minimum(i, nvar - 1), 0, 0, 0))]),
        out_specs=pl.BlockSpec((1, tq, W_B), lambda bb, i: (bb, i, 0)),
        out_shape=jax.ShapeDtypeStruct((b, t, W_B), BF16),
        compiler_params=_cp("arbitrary", "arbitrary"),
        name="mixer_b",
    )(q2, *([kv] * (2 * nkb)), bias)


def _merge_kernel(h_ref, oa_ref, ob_ref, pa_ref, pb_ref, wga_ref, wgb_ref, bga_ref, bgb_ref, o_ref):
    h = h_ref[...]
    ga = jax.nn.sigmoid(jnp.dot(h, wga_ref[...], preferred_element_type=F32) + bga_ref[...])
    gb = jax.nn.sigmoid(jnp.dot(h, wgb_ref[...], preferred_element_type=F32) + bgb_ref[...])
    ya = jnp.dot(oa_ref[...], pa_ref[...], preferred_element_type=F32)
    yb = jnp.dot(ob_ref[...], pb_ref[...], preferred_element_type=F32)
    o_ref[...] = (ga * ya + gb * yb).astype(o_ref.dtype)


def _merge(h, oa, ob, pa, pb, wg, bg):
    m, d = h.shape
    tm = _pick(m, 512, 16)
    tn = _pick(d, 512, LANES)
    nb = d // tn
    bg2 = bg.reshape(1, 2 * d)
    return pl.pallas_call(
        _merge_kernel,
        grid=(m // tm, nb),
        in_specs=[pl.BlockSpec((tm, d), lambda i, j: (i, 0)),
                  pl.BlockSpec((tm, W_A), lambda i, j: (i, 0)),
                  pl.BlockSpec((tm, W_B), lambda i, j: (i, 0)),
                  pl.BlockSpec((W_A, tn), lambda i, j: (0, j)),
                  pl.BlockSpec((W_B, tn), lambda i, j: (0, j)),
                  pl.BlockSpec((d, tn), lambda i, j: (0, j)),
                  pl.BlockSpec((d, tn), lambda i, j: (0, j + nb)),
                  pl.BlockSpec((1, tn), lambda i, j: (0, j)),
                  pl.BlockSpec((1, tn), lambda i, j: (0, j + nb))],
        out_specs=pl.BlockSpec((tm, tn), lambda i, j: (i, j)),
        out_shape=jax.ShapeDtypeStruct((m, d), BF16),
        compiler_params=_cp("arbitrary", "arbitrary"),
        name="merge",
    )(h, oa, ob, pa, pb, wg, wg, bg2, bg2)


def _outproj_kernel(mg_ref, x_ref, w_ref, mod_ref, g_ref, x1_ref, h2_ref):
    o = jnp.dot(mg_ref[0], w_ref[...], preferred_element_type=F32)
    x1 = x_ref[0] + mod_ref[0, 2:3, :] * o
    x1_ref[0] = x1
    h2_ref[0] = _rms_mod(x1, g_ref[...], mod_ref[0, 3:4, :], mod_ref[0, 4:5, :]).astype(h2_ref.dtype)


def _outproj(merged, x, w_out, mod, g2):
    b, t, d = x.shape
    tm = _pick(t, 256, 16)
    row = lambda i, j: (i, j, 0)
    return pl.pallas_call(
        _outproj_kernel,
        grid=(b, t // tm),
        in_specs=[pl.BlockSpec((1, tm, d), row),
                  pl.BlockSpec((1, tm, d), row),
                  pl.BlockSpec((d, d), lambda i, j: (0, 0)),
                  pl.BlockSpec((1, 6, d), lambda i, j: (i, 0, 0)),
                  pl.BlockSpec((1, d), lambda i, j: (0, 0))],
        out_specs=[pl.BlockSpec((1, tm, d), row), pl.BlockSpec((1, tm, d), row)],
        out_shape=[jax.ShapeDtypeStruct((b, t, d), F32), jax.ShapeDtypeStruct((b, t, d), BF16)],
        compiler_params=_cp("arbitrary", "arbitrary"),
        name="outproj",
    )(merged, x, w_out, mod, g2.reshape(1, d))


def _ffn_kernel(h2_ref, x1_ref, mod_ref, wa_ref, wb_ref, wd_ref, wc_ref, bc_ref, prev_ref, fg_ref,
                y_ref, st_ref, abuf, carry, acc_sc, *, tm, nf):
    m = pl.program_id(1)
    f = pl.program_id(2)

    @pl.when(f == 0)
    def _():
        acc_sc[...] = jnp.zeros(acc_sc.shape, F32)

    h2 = h2_ref[0]
    prev = jnp.where(m == 0, prev_ref[0], carry[f])
    tf = wa_ref.shape[1]
    nsub = 2 if tf % (2 * MXU_COLS) == 0 else 1
    tfs = tf // nsub

    def up(c):
        cols = slice(c * tfs, (c + 1) * tfs)
        return (jnp.dot(h2, wa_ref[:, cols], preferred_element_type=F32),
                jnp.dot(h2, wb_ref[:, cols], preferred_element_type=F32))

    nxt = up(0)
    for c in range(nsub):
        cols = slice(c * tfs, (c + 1) * tfs)
        a, gate = nxt
        if c + 1 < nsub:
            nxt = up(c + 1)
        abuf[6:8, cols] = prev[:, cols]
        abuf[8:8 + tm, cols] = a
        tail = a[tm - 2:tm, :]
        carry[f, :, cols] = tail
        st_ref[0, 0, :, cols] = tail
        conv = (bc_ref[:, cols] + wc_ref[0:1, cols] * abuf[6:6 + tm, cols]
                + wc_ref[1:2, cols] * abuf[7:7 + tm, cols] + wc_ref[2:3, cols] * a)
        u = (conv * jax.nn.sigmoid(conv) * gate).astype(BF16)
        acc_sc[...] += jnp.dot(u, wd_ref[cols, :], preferred_element_type=F32)

    @pl.when(f == nf - 1)
    def _():
        x2 = x1_ref[0] + mod_ref[0, 5:6, :] * acc_sc[...]
        y_ref[0] = x2 * lax.rsqrt(jnp.mean(x2 * x2, axis=-1, keepdims=True) + EPS) * fg_ref[...]


def _ffn(h2, x1, mod, wa, wb, wd, wconv, bconv, prev, final_g):
    b, t, d = x1.shape
    dff = wa.shape[1]
    tm = _pick(t, 512, 16)
    tf = _pick(dff, 512, LANES)
    nf = dff // tf
    row = lambda i, j, k: (i, j, 0)
    kernel = functools.partial(_ffn_kernel, tm=tm, nf=nf)
    return pl.pallas_call(
        kernel,
        grid=(b, t // tm, nf),
        in_specs=[pl.BlockSpec((1, tm, d), row),
                  pl.BlockSpec((1, tm, d), row),
                  pl.BlockSpec((1, 6, d), lambda i, j, k: (i, 0, 0)),
                  pl.BlockSpec((d, tf), lambda i, j, k: (0, k)),
                  pl.BlockSpec((d, tf), lambda i, j, k: (0, k)),
                  pl.BlockSpec((tf, d), lambda i, j, k: (k, 0)),
                  pl.BlockSpec((CONV_W, tf), lambda i, j, k: (0, k)),
                  pl.BlockSpec((1, tf), lambda i, j, k: (0, k)),
                  pl.BlockSpec((1, CONV_W - 1, tf), lambda i, j, k: (i, 0, k)),
                  pl.BlockSpec((1, d), lambda i, j, k: (0, 0))],
        out_specs=[pl.BlockSpec((1, tm, d), row),
                   pl.BlockSpec((1, 1, CONV_W - 1, tf), lambda i, j, k: (i, j, 0, k))],
        out_shape=[jax.ShapeDtypeStruct((b, t, d), F32),
                   jax.ShapeDtypeStruct((b, t // tm, CONV_W - 1, dff), F32)],
        scratch_shapes=[pltpu.VMEM((tm + 8, tf), F32),
                        pltpu.VMEM((nf, CONV_W - 1, tf), F32),
                        pltpu.VMEM((tm, d), F32)],
        compiler_params=_cp("arbitrary", "arbitrary", "arbitrary"),
        name="ffn",
    )(h2, x1, mod, wa, wb, wd, wconv, bconv.reshape(1, dff), prev, final_g.reshape(1, d))


def _prep_weights(w_in, w_gate, w_proj_a, w_proj_b, w_out, w_up, w_down):
    o = np.cumsum((0, W_A, W_A, W_A, N_IDX_HEADS * IDX_DIM, IDX_DIM, N_IDX_HEADS, W_B, W_B, W_B))
    col = lambda a, b: w_in[:, int(o[a]):int(o[b])]
    d = w_in.shape[0]
    dff = w_down.shape[0]
    pad = jnp.zeros((d, LANES - IDX_DIM - N_IDX_HEADS), w_in.dtype)
    qi_hi, qi_lo = _split_bf16(col(3, 4))
    kiwi_hi, kiwi_lo = _split_bf16(jnp.concatenate([col(4, 6), pad], axis=1))
    return dict(
        q2=jnp.concatenate([col(0, 1), col(6, 7)], axis=1).astype(BF16),
        ka=col(1, 2).astype(BF16),
        va=col(2, 3).astype(BF16),
        qi_hi=qi_hi, qi_lo=qi_lo, kiwi_hi=kiwi_hi, kiwi_lo=kiwi_lo,
        kvb=col(7, 9).astype(BF16),
        gate=w_gate.astype(BF16),
        pa=w_proj_a.astype(BF16),
        pb=w_proj_b.astype(BF16),
        out=w_out.astype(BF16),
        up_a=w_up[:, :dff].astype(BF16),
        up_b=w_up[:, dff:].astype(BF16),
        down=w_down.astype(BF16),
    )


def _trunk_layer(x, mod, cache, wts, norm1_g, rel_bias, b_gate, norm2_g, w_conv, b_conv, final_g):
    b, t, d = x.shape
    dff = wts["down"].shape[0]
    h, h_lo = _norm_mod(x, norm1_g, mod)
    h2d = h.reshape(b * t, d)
    hl2d = h_lo.reshape(b * t, d)
    (q2,) = _mm(h2d, wts["q2"], (BF16,))
    ka32, ka16 = _mm(h2d, wts["ka"], (F32, BF16))
    va32, va16 = _mm(h2d, wts["va"], (F32, BF16))
    qi = _mm_split(h2d, hl2d, wts["qi_hi"], wts["qi_lo"]).reshape(b, t, N_IDX_HEADS * IDX_DIM)
    kiwi = _mm_split(h2d, hl2d, wts["kiwi_hi"], wts["kiwi_lo"])
    (kvb16,) = _mm(h2d, wts["kvb"], (BF16,))
    q2 = q2.reshape(b, t, W_A + W_B)
    kiwi = kiwi.reshape(b, t, LANES)
    ki = kiwi[:, :, :IDX_DIM]
    wi = kiwi[:, :, IDX_DIM:IDX_DIM + N_IDX_HEADS]
    ka16 = ka16.reshape(b, t, W_A)
    va16 = va16.reshape(b, t, W_A)
    kvb16 = kvb16.reshape(b, t, 2 * W_B)

    if cache is None:
        rows_tail = min(BAND_PAST, t)
        tq_a = _pick(t, 256, CHUNK)
        tk_a = _pick(t, 512, LANES)
        o_a = _mixer_a(q2, qi, wi, ki, ka16, va16, past=0, valid_len=t, topk=min(TOPK_MAX, t // 4),
                       tq=tq_a, tk=tk_a)
        tq_b = _pick(t, 256, CHUNK)
        assert BAND_PAST % tq_b == 0
        back = BAND_PAST // tq_b
        lows = [BAND_PAST - v * tq_b for v in range(back + 1)]
        bias = _band_bias(rel_bias, tq_b, BAND_PAST + tq_b, BAND_PAST, lows, BAND_PAST + tq_b)
        o_b = _mixer_b(q2, kvb16, bias, tq=tq_b, tkb=tq_b, nkb=back + 1, back=back)
        conv_prev = jnp.zeros((b, CONV_W - 1, dff), F32)
    else:
        ck, cv, cki, cbk, cbv, conv_prev = cache
        past = ck.shape[1]
        rows_tail = t
        tk_a = LANES
        l_valid = past + t
        lp = -(-l_valid // tk_a) * tk_a
        padk = jnp.zeros((b, lp - l_valid, W_A), BF16)
        k_all = jnp.concatenate([ck.reshape(b, past, W_A).astype(BF16), ka16, padk], axis=1)
        v_all = jnp.concatenate([cv.reshape(b, past, W_A).astype(BF16), va16, padk], axis=1)
        ki_all = jnp.concatenate([cki, ki, jnp.zeros((b, lp - l_valid, IDX_DIM), F32)], axis=1)
        tq_a = -(-t // LANES) * LANES
        padq = lambda a: jnp.pad(a, ((0, 0), (0, tq_a - t), (0, 0)))
        o_a = _mixer_a(padq(q2), padq(qi), padq(wi), ki_all, k_all, v_all, past=past, valid_len=l_valid,
                       topk=min(TOPK_MAX, l_valid // 4), tq=tq_a, tk=tk_a)[:, :t]
        rows = cbk.shape[1]
        assert past % CHUNK == 0 and rows % CHUNK == 0
        lb_valid = rows + t
        lb = -(-lb_valid // LANES) * LANES
        kv_cache = jnp.concatenate([cbk.reshape(b, rows, W_B), cbv.reshape(b, rows, W_B)], axis=2).astype(BF16)
        kv_all = jnp.concatenate([kv_cache, kvb16, jnp.zeros((b, lb - lb_valid, 2 * W_B), BF16)], axis=1)
        bias = _band_bias(rel_bias, t, lb, rows, [0], lb_valid)
        o_b = _mixer_b(q2, kv_all, bias, tq=t, tkb=lb, nkb=1, back=0)

    (tail32,) = _mm(h[:, t - rows_tail:].reshape(b * rows_tail, d), wts["kvb"], (F32,))
    tail32 = tail32.reshape(b, rows_tail, 2, N_HEADS_B, HEAD_DIM)
    new_bk, new_bv = tail32[:, :, 0], tail32[:, :, 1]

    merged = _merge(h2d, o_a.reshape(b * t, W_A), o_b.reshape(b * t, W_B), wts["pa"], wts["pb"],
                    wts["gate"], b_gate)
    x1, h2 = _outproj(merged.reshape(b, t, d), x, wts["out"], mod, norm2_g)
    y, tails = _ffn(h2, x1, mod, wts["up_a"], wts["up_b"], wts["down"], w_conv, b_conv,
                    conv_prev, final_g)
    conv_state = tails[:, -1]
    state = (ka32.reshape(b, t, N_HEADS_A, HEAD_DIM), va32.reshape(b, t, N_HEADS_A, HEAD_DIM), ki,
             new_bk, new_bv, conv_state)
    return y, state


def kernel(x_prompt, x_sample, cache_a_k, cache_a_v, cache_idx_k, cache_b_k, cache_b_v, state_ffn_conv,
           c_prompt, c_sample, w_ada, b_ada, norm1_g, w_in, rel_bias, w_gate, b_gate, w_proj_a, w_proj_b,
           w_out, norm2_g, w_up, w_conv, b_conv, w_down, final_g):
    depth = w_ada.shape[0]
    assert depth == 1, "the fused final RMSNorm assumes a single layer"
    d = x_prompt.shape[-1]
    nb_p = x_prompt.shape[0]
    c_all = jnp.concatenate([c_prompt, c_sample], axis=0)
    xp, xs = x_prompt, x_sample
    states_p, states_s = [], []
    for l in range(depth):
        wts = _prep_weights(w_in[l], w_gate[l], w_proj_a[l], w_proj_b[l], w_out[l], w_up[l], w_down[l])
        mod = _ada(c_all, w_ada[l], b_ada[l]).reshape(c_all.shape[0], 6, d)
        args = (wts, norm1_g[l], rel_bias[l], b_gate[l], norm2_g[l], w_conv[l], b_conv[l], final_g)
        xp, st_p = _trunk_layer(xp, mod[:nb_p], None, *args)
        states_p.append(st_p)
        cache_l = (cache_a_k[l], cache_a_v[l], cache_idx_k[l], cache_b_k[l], cache_b_v[l], state_ffn_conv[l])
        xs, st_s = _trunk_layer(xs, mod[nb_p:], cache_l, *args)
        states_s.append(st_s)
    sp = [t[0][None] for t in zip(*states_p)]
    ss = [t[0][None] for t in zip(*states_s)]
    return (xp, xs, *sp, *ss)
```

```python
import functools

import jax
import jax.numpy as jnp
import numpy as np
from jax import lax
from jax.experimental import pallas as pl
from jax.experimental.pallas import tpu as pltpu

F32 = jnp.float32
BF16 = jnp.bfloat16

CHUNK = 64
CHUNK_SHIFT = 6
HEAD_DIM = 128
N_HEADS_A = 8
N_IDX_HEADS = 16
IDX_DIM = 64
TOPK_MAX = 256
N_HEADS_B = 8
BAND_CHUNKS = 8
BAND_PAST = BAND_CHUNKS * CHUNK
REL_CLIP = 2 * CHUNK
CONV_W = 3
EPS = 1e-6
NEG = -1e30
W_A = N_HEADS_A * HEAD_DIM
W_B = N_HEADS_B * HEAD_DIM
ATTN_SCALE = HEAD_DIM ** -0.5
LOG2E = 1.4426950408889634
HALF_BITS = 16
HALF_MASK = 0xFFFF
HALF_BIAS = 1 << (HALF_BITS - 1)
LANES = 128
MXU_COLS = 256
VMEM_LIMIT = 56 * 1024 * 1024


def _cp(*sem):
    return pltpu.CompilerParams(dimension_semantics=sem, vmem_limit_bytes=VMEM_LIMIT)


def _pick(dim, pref, align):
    t = min(pref, dim)
    t -= t % align
    while t >= align:
        if dim % t == 0:
            return t
        t -= align
    return dim


def _ada_kernel(c_ref, w_ref, b_ref, o_ref):
    c = c_ref[...]
    s = c * jax.nn.sigmoid(c)
    o_ref[...] = jnp.dot(s, w_ref[...], preferred_element_type=F32,
                         precision=lax.Precision.HIGHEST) + b_ref[...]


def _ada(c, w, b):
    r, d = c.shape
    n = w.shape[1]
    tn = _pick(n, 1024, LANES)
    return pl.pallas_call(
        _ada_kernel,
        grid=(n // tn,),
        in_specs=[pl.BlockSpec((r, d), lambda j: (0, 0)),
                  pl.BlockSpec((d, tn), lambda j: (0, j)),
                  pl.BlockSpec((1, tn), lambda j: (0, j))],
        out_specs=pl.BlockSpec((r, tn), lambda j: (0, j)),
        out_shape=jax.ShapeDtypeStruct((r, n), F32),
        compiler_params=_cp("arbitrary"),
        name="ada",
    )(c, w, b.reshape(1, n))


def _rms_mod(x, g, shift, scale):
    y = x * lax.rsqrt(jnp.mean(x * x, axis=-1, keepdims=True) + EPS) * g
    return y * (1.0 + scale) + shift


def _split_bf16(x):
    hi = x.astype(BF16)
    return hi, (x - hi.astype(F32)).astype(BF16)


def _norm_mod_kernel(x_ref, g_ref, mod_ref, hi_ref, lo_ref):
    h = _rms_mod(x_ref[0], g_ref[...], mod_ref[0, 0:1, :], mod_ref[0, 1:2, :])
    hi, lo = _split_bf16(h)
    hi_ref[0] = hi
    lo_ref[0] = lo


def _norm_mod(x, g, mod):
    b, t, d = x.shape
    tm = _pick(t, 512, 16)
    row = lambda i, j: (i, j, 0)
    return pl.pallas_call(
        _norm_mod_kernel,
        grid=(b, t // tm),
        in_specs=[pl.BlockSpec((1, tm, d), row),
                  pl.BlockSpec((1, d), lambda i, j: (0, 0)),
                  pl.BlockSpec((1, 6, d), lambda i, j: (i, 0, 0))],
        out_specs=[pl.BlockSpec((1, tm, d), row), pl.BlockSpec((1, tm, d), row)],
        out_shape=[jax.ShapeDtypeStruct((b, t, d), BF16), jax.ShapeDtypeStruct((b, t, d), BF16)],
        compiler_params=_cp("arbitrary", "arbitrary"),
        name="norm_mod",
    )(x, g.reshape(1, d), mod)


def _mm_kernel(a_ref, w_ref, *o_refs):
    acc = jnp.dot(a_ref[...], w_ref[...], preferred_element_type=F32)
    for o_ref in o_refs:
        o_ref[...] = acc.astype(o_ref.dtype)


def _mm(a, w, dtypes):
    m, k = a.shape
    n = w.shape[1]
    tm = _pick(m, 1024, 16)
    tn = _pick(n, 1024, LANES)
    outs = pl.pallas_call(
        _mm_kernel,
        grid=(m // tm, n // tn),
        in_specs=[pl.BlockSpec((tm, k), lambda i, j: (i, 0)),
                  pl.BlockSpec((k, tn), lambda i, j: (0, j))],
        out_specs=[pl.BlockSpec((tm, tn), lambda i, j: (i, j)) for _ in dtypes],
        out_shape=[jax.ShapeDtypeStruct((m, n), dt) for dt in dtypes],
        compiler_params=_cp("arbitrary", "arbitrary"),
        name="proj",
    )(a, w)
    return outs


def _mm_split_kernel(ah_ref, al_ref, wh_ref, wl_ref, o_ref):
    ah = ah_ref[...]
    acc = jnp.dot(ah, wh_ref[...], preferred_element_type=F32)
    acc = acc + jnp.dot(al_ref[...], wh_ref[...], preferred_element_type=F32)
    acc = acc + jnp.dot(ah, wl_ref[...], preferred_element_type=F32)
    o_ref[...] = acc


def _mm_split(a_hi, a_lo, w_hi, w_lo):
    m, k = a_hi.shape
    n = w_hi.shape[1]
    tm = _pick(m, 1024, 16)
    tn = _pick(n, 1024, LANES)
    a_spec = pl.BlockSpec((tm, k), lambda i, j: (i, 0))
    w_spec = pl.BlockSpec((k, tn), lambda i, j: (0, j))
    return pl.pallas_call(
        _mm_split_kernel,
        grid=(m // tm, n // tn),
        in_specs=[a_spec, a_spec, w_spec, w_spec],
        out_specs=pl.BlockSpec((tm, tn), lambda i, j: (i, j)),
        out_shape=jax.ShapeDtypeStruct((m, n), F32),
        compiler_params=_cp("arbitrary", "arbitrary"),
        name="proj_split",
    )(a_hi, a_lo, w_hi, w_lo)


def _mixer_a_kernel(tile_ref, block_ref, qa_ref, qi_ref, wit_ref, ki2_ref, k_ref, vt_ref, o_ref,
                    key_sc, hi_sc, lo_sc, thr_sc, m_sc, l_sc, acc_sc, qx_sc,
                    *, tq, tk, past, valid_len, topk):
    i = tile_ref[pl.program_id(1)]
    j = block_ref[pl.program_id(1)]
    q0 = past + i * tq
    kend = jnp.minimum(((q0 + tq + CHUNK - 1) // CHUNK) * CHUNK, valid_len)
    nblk = (kend + tk - 1) // tk
    slab16 = 16

    def admissible(jb):
        kpos = jb * tk + lax.broadcasted_iota(jnp.int32, (tk, tq), 0)
        qpos = q0 + lax.broadcasted_iota(jnp.int32, (tk, tq), 1)
        adm = (jnp.right_shift(kpos, CHUNK_SHIFT) <= jnp.right_shift(qpos, CHUNK_SHIFT)) & (kpos < valid_len)
        return adm, qpos, kpos

    @pl.when(j == 0)
    def _index_and_threshold():
        qt = qi_ref[0].T
        q_hi, q_lo = _split_bf16(qt)
        for h in range(N_IDX_HEADS):
            rows = slice(h * IDX_DIM, (h + 1) * IDX_DIM)
            qx_sc[h, 0 * IDX_DIM:1 * IDX_DIM, :] = q_hi[rows]
            qx_sc[h, 1 * IDX_DIM:2 * IDX_DIM, :] = q_hi[rows]
            qx_sc[h, 2 * IDX_DIM:3 * IDX_DIM, :] = q_lo[rows]
            qx_sc[h, 3 * IDX_DIM:4 * IDX_DIM, :] = q_lo[rows]

        first = lax.broadcasted_iota(jnp.int32, (tk, LANES), 1) < IDX_DIM

        def score_block(jb, carry):
            dup = ki2_ref[0, jb]
            kk = jnp.where(first, dup, dup - dup.astype(BF16).astype(F32)).astype(BF16)
            kx = jnp.concatenate([kk, kk], axis=1)
            acc = jnp.zeros((tk, tq), F32)
            for h in range(N_IDX_HEADS):
                rel = jnp.dot(kx, qx_sc[h], preferred_element_type=F32)
                acc = acc + wit_ref[0, h:h + 1, :] * jnp.maximum(rel, 0.0)
            adm, _, _ = admissible(jb)
            acc = jnp.where(adm, acc, NEG)
            bits = pltpu.bitcast(acc, jnp.int32)
            key = bits ^ (jnp.right_shift(bits, 31) & 0x7FFFFFFF)
            key_sc[jb] = key
            hi_sc[jb] = jnp.right_shift(key, HALF_BITS).astype(jnp.int16)
            lo_sc[jb] = ((key & HALF_MASK) - HALF_BIAS).astype(jnp.int16)
            return carry

        lax.fori_loop(0, nblk, score_block, 0)

        def count_ge(src_sc, cand):
            cand16 = cand.astype(jnp.int16)

            def count_block(jb, cnt):
                for r in range(tk // slab16):
                    rows = slice(r * slab16, (r + 1) * slab16)
                    cnt = cnt + jnp.where(src_sc[jb, rows, :] >= cand16, jnp.int16(1), jnp.int16(0))
                return cnt

            cnt = lax.fori_loop(0, nblk, count_block, jnp.zeros((slab16, tq), jnp.int16))
            return jnp.sum(cnt.astype(jnp.int32), axis=0, keepdims=True)

        def kth_largest16(src_sc, want):
            def bit_step(b, t_u):
                cand_u = t_u | lax.shift_left(jnp.int32(1), HALF_BITS - 1 - b)
                return jnp.where(count_ge(src_sc, cand_u - HALF_BIAS) >= want, cand_u, t_u)
            t_u = lax.fori_loop(0, HALF_BITS, bit_step, jnp.zeros((slab16, tq), jnp.int32))
            return t_u - HALF_BIAS

        t_hi = kth_largest16(hi_sc, topk)
        above = jnp.where(t_hi == HALF_BIAS - 1, 0, count_ge(hi_sc, jnp.minimum(t_hi + 1, HALF_BIAS - 1)))
        t_hi16 = t_hi.astype(jnp.int16)

        def keep_group(jb, carry):
            for r in range(tk // slab16):
                rows = slice(r * slab16, (r + 1) * slab16)
                lo_sc[jb, rows, :] = jnp.where(hi_sc[jb, rows, :] == t_hi16, lo_sc[jb, rows, :],
                                               jnp.int16(-HALF_BIAS))
            return carry

        lax.fori_loop(0, nblk, keep_group, 0)
        t_lo = kth_largest16(lo_sc, topk - above)
        thr_sc[...] = (t_hi * (2 * HALF_BIAS) + (t_lo + HALF_BIAS))[0:8, :]
        m_sc[...] = jnp.full(m_sc.shape, -jnp.inf, F32)
        l_sc[...] = jnp.zeros(l_sc.shape, F32)
        acc_sc[...] = jnp.zeros(acc_sc.shape, F32)

    def _attend():
        adm, qpos, kpos = admissible(j)
        sel = (key_sc[j] >= thr_sc[0:1, :]) & adm
        dist = jnp.where(sel, jnp.abs(qpos - kpos).astype(F32), -NEG)

        def logits(h):
            hs = slice(h * HEAD_DIM, (h + 1) * HEAD_DIM)
            return lax.dot_general(k_ref[0, :, hs], qa_ref[0, :, hs], (((1,), (1,)), ((), ())),
                                   preferred_element_type=F32)

        s_next = logits(0)
        for h in range(N_HEADS_A):
            hs = slice(h * HEAD_DIM, (h + 1) * HEAD_DIM)
            s_raw = s_next
            if h + 1 < N_HEADS_A:
                s_next = logits(h + 1)
            s = s_raw * (ATTN_SCALE * LOG2E) - ((2.0 ** -(h + 1)) * LOG2E) * dist
            m_prev = m_sc[h:h + 1, :]
            m_new = jnp.maximum(m_prev, jnp.max(s, axis=0, keepdims=True))
            alpha = jnp.exp2(m_prev - m_new)
            p = jnp.exp2(s - m_new)
            l_sc[h:h + 1, :] = alpha * l_sc[h:h + 1, :] + jnp.sum(p, axis=0, keepdims=True)
            acc_sc[hs, :] = alpha * acc_sc[hs, :] + jnp.dot(vt_ref[0, hs, :], p.astype(BF16),
                                                           preferred_element_type=F32)
            m_sc[h:h + 1, :] = m_new

    _attend()

    @pl.when(j == nblk - 1)
    def _finalize():
        for h in range(N_HEADS_A):
            hs = slice(h * HEAD_DIM, (h + 1) * HEAD_DIM)
            o_ref[0, :, hs] = (acc_sc[hs, :] / l_sc[h:h + 1, :]).T.astype(o_ref.dtype)


def _mixer_a(q2, qi, wi, ki_all, k_all, v_all, *, past, valid_len, topk, tq, tk):
    b, t, _ = q2.shape
    lp = k_all.shape[1]
    nkb = lp // tk
    assert lp % tk == 0 and t % tq == 0 and tk % LANES == 0 and past % CHUNK == 0
    ki2 = jnp.concatenate([ki_all, ki_all], axis=2).reshape(b, nkb, tk, 2 * IDX_DIM)
    wit = jnp.swapaxes(wi, 1, 2)
    vt = jnp.swapaxes(v_all, 1, 2)

    def blocks_needed(i):
        kend = min(-(-(past + (i + 1) * tq) // CHUNK) * CHUNK, valid_len)
        return -(-kend // tk)

    pairs = [(i, j) for i in range(t // tq) for j in range(blocks_needed(i))]
    tile_of = jnp.asarray([p[0] for p in pairs], jnp.int32)
    block_of = jnp.asarray([p[1] for p in pairs], jnp.int32)
    q_map = lambda bb, p, ti, bj: (bb, ti[p], 0)
    kernel = functools.partial(_mixer_a_kernel, tq=tq, tk=tk, past=past, valid_len=valid_len, topk=topk)
    grid_spec = pltpu.PrefetchScalarGridSpec(
        num_scalar_prefetch=2,
        grid=(b, len(pairs)),
        in_specs=[pl.BlockSpec((1, tq, W_A), q_map),
                  pl.BlockSpec((1, tq, N_IDX_HEADS * IDX_DIM), q_map),
                  pl.BlockSpec((1, N_IDX_HEADS, tq), lambda bb, p, ti, bj: (bb, 0, ti[p])),
                  pl.BlockSpec((1, nkb, tk, 2 * IDX_DIM), lambda bb, p, ti, bj: (bb, 0, 0, 0)),
                  pl.BlockSpec((1, tk, W_A), lambda bb, p, ti, bj: (bb, bj[p], 0)),
                  pl.BlockSpec((1, W_A, tk), lambda bb, p, ti, bj: (bb, 0, bj[p]))],
        out_specs=pl.BlockSpec((1, tq, W_A), q_map),
        scratch_shapes=[pltpu.VMEM((nkb, tk, tq), jnp.int32),
                        pltpu.VMEM((nkb, tk, tq), jnp.int16),
                        pltpu.VMEM((nkb, tk, tq), jnp.int16),
                        pltpu.VMEM((8, tq), jnp.int32),
                        pltpu.VMEM((N_HEADS_A, tq), F32),
                        pltpu.VMEM((N_HEADS_A, tq), F32),
                        pltpu.VMEM((W_A, tq), F32),
                        pltpu.VMEM((N_IDX_HEADS, 4 * IDX_DIM, tq), BF16)])
    return pl.pallas_call(
        kernel,
        grid_spec=grid_spec,
        out_shape=jax.ShapeDtypeStruct((b, t, W_A), BF16),
        compiler_params=_cp("arbitrary", "arbitrary"),
        name="mixer_a",
    )(tile_of, block_of, q2, qi, wit, ki2, k_all, vt)


def _mixer_b_kernel(*refs, nkb, tkb):
    q_ref = refs[0]
    k_refs = refs[1:1 + nkb]
    v_refs = refs[1 + nkb:1 + 2 * nkb]
    bias_ref = refs[1 + 2 * nkb]
    o_ref = refs[2 + 2 * nkb]
    for h in range(N_HEADS_B):
        hs = slice(h * HEAD_DIM, (h + 1) * HEAD_DIM)
        q = q_ref[0, :, hs]
        parts = [lax.dot_general(q, kr[0, :, hs], (((1,), (1,)), ((), ())), preferred_element_type=F32)
                 for kr in k_refs]
        s = parts[0] if nkb == 1 else jnp.concatenate(parts, axis=1)
        s = s * ATTN_SCALE + bias_ref[0, h]
        p = jnp.exp(s - jnp.max(s, axis=1, keepdims=True))
        l = jnp.sum(p, axis=1, keepdims=True)
        pb = p.astype(BF16)
        pv = None
        for c, vr in enumerate(v_refs):
            t = jnp.dot(pb[:, c * tkb:(c + 1) * tkb], vr[0, :, hs], preferred_element_type=F32)
            pv = t if pv is None else pv + t
        o_ref[0, :, hs] = (pv / l).astype(o_ref.dtype)


def _band_bias(rel_bias, tq, wk, off, lows, hi):
    nh = rel_bias.shape[0]
    n = wk + tq
    e = np.concatenate([np.arange(wk + 1), np.arange(-(tq - 1), 0)])
    f = rel_bias[:, np.clip(off - e, -REL_CLIP, REL_CLIP) + REL_CLIP].astype(F32)
    bias = jnp.tile(f, (1, tq))[:, :tq * (n - 1)].reshape(nh, tq, n - 1)[:, :, :wk]
    r = np.arange(tq)[:, None]
    c = np.arange(wk)[None, :]
    dq = r // CHUNK
    dk = np.floor_divide(c - off, CHUNK)
    ok = (dk <= dq) & (dk >= dq - BAND_CHUNKS) & (c < hi)
    tiles = [jnp.where((ok & (c >= lo))[None], bias, NEG) for lo in lows]
    return jnp.stack(tiles)


def _mixer_b(q2, kv, bias, *, tq, tkb, nkb, back):
    b, t, _ = q2.shape
    nvar = bias.shape[0]
    wk = nkb * tkb
    kmaps = [functools.partial(lambda bb, i, p, col: (bb, jnp.maximum(i - back + p, 0), col), p=p, col=0)
             for p in range(nkb)]
    vmaps = [functools.partial(lambda bb, i, p, col: (bb, jnp.maximum(i - back + p, 0), col), p=p, col=1)
             for p in range(nkb)]
    kernel = functools.partial(_mixer_b_kernel, nkb=nkb, tkb=tkb)
    return pl.pallas_call(
        kernel,
        grid=(b, t // tq),
        in_specs=([pl.BlockSpec((1, tq, W_B), lambda bb, i: (bb, i, 1))]
                  + [pl.BlockSpec((1, tkb, W_B), m) for m in kmaps]
                  + [pl.BlockSpec((1, tkb, W_B), m) for m in vmaps]
                  + [pl.BlockSpec((1, N_HEADS_B, tq, wk), lambda bb, i: (jnp.minimum(i, nvar - 1), 0, 0, 0))]),
        out_specs=pl.BlockSpec((1, tq, W_B), lambda bb, i: (bb, i, 0)),
        out_shape=jax.ShapeDtypeStruct((b, t, W_B), BF16),
        compiler_params=_cp("arbitrary", "arbitrary"),
        name="mixer_b",
    )(q2, *([kv] * (2 * nkb)), bias)


def _merge_kernel(h_ref, oa_ref, ob_ref, pa_ref, pb_ref, wga_ref, wgb_ref, bga_ref, bgb_ref, o_ref):
    h = h_ref[...]
    ga = jax.nn.sigmoid(jnp.dot(h, wga_ref[...], preferred_element_type=F32) + bga_ref[...])
    gb = jax.nn.sigmoid(jnp.dot(h, wgb_ref[...], preferred_element_type=F32) + bgb_ref[...])
    ya = jnp.dot(oa_ref[...], pa_ref[...], preferred_element_type=F32)
    yb = jnp.dot(ob_ref[...], pb_ref[...], preferred_element_type=F32)
    o_ref[...] = (ga * ya + gb * yb).astype(o_ref.dtype)


def _merge(h, oa, ob, pa, pb, wg, bg):
    m, d = h.shape
    tm = _pick(m, 512, 16)
    tn = _pick(d, 512, LANES)
    nb = d // tn
    bg2 = bg.reshape(1, 2 * d)
    return pl.pallas_call(
        _merge_kernel,
        grid=(m // tm, nb),
        in_specs=[pl.BlockSpec((tm, d), lambda i, j: (i, 0)),
                  pl.BlockSpec((tm, W_A), lambda i, j: (i, 0)),
                  pl.BlockSpec((tm, W_B), lambda i, j: (i, 0)),
                  pl.BlockSpec((W_A, tn), lambda i, j: (0, j)),
                  pl.BlockSpec((W_B, tn), lambda i, j: (0, j)),
                  pl.BlockSpec((d, tn), lambda i, j: (0, j)),
                  pl.BlockSpec((d, tn), lambda i, j: (0, j + nb)),
                  pl.BlockSpec((1, tn), lambda i, j: (0, j)),
                  pl.BlockSpec((1, tn), lambda i, j: (0, j + nb))],
        out_specs=pl.BlockSpec((tm, tn), lambda i, j: (i, j)),
        out_shape=jax.ShapeDtypeStruct((m, d), BF16),
        compiler_params=_cp("arbitrary", "arbitrary"),
        name="merge",
    )(h, oa, ob, pa, pb, wg, wg, bg2, bg2)


def _outproj_kernel(mg_ref, x_ref, w_ref, mod_ref, g_ref, x1_ref, h2_ref):
    o = jnp.dot(mg_ref[0], w_ref[...], preferred_element_type=F32)
    x1 = x_ref[0] + mod_ref[0, 2:3, :] * o
    x1_ref[0] = x1
    h2_ref[0] = _rms_mod(x1, g_ref[...], mod_ref[0, 3:4, :], mod_ref[0, 4:5, :]).astype(h2_ref.dtype)


def _outproj(merged, x, w_out, mod, g2):
    b, t, d = x.shape
    tm = _pick(t, 256, 16)
    row = lambda i, j: (i, j, 0)
    return pl.pallas_call(
        _outproj_kernel,
        grid=(b, t // tm),
        in_specs=[pl.BlockSpec((1, tm, d), row),
                  pl.BlockSpec((1, tm, d), row),
                  pl.BlockSpec((d, d), lambda i, j: (0, 0)),
                  pl.BlockSpec((1, 6, d), lambda i, j: (i, 0, 0)),
                  pl.BlockSpec((1, d), lambda i, j: (0, 0))],
        out_specs=[pl.BlockSpec((1, tm, d), row), pl.BlockSpec((1, tm, d), row)],
        out_shape=[jax.ShapeDtypeStruct((b, t, d), F32), jax.ShapeDtypeStruct((b, t, d), BF16)],
        compiler_params=_cp("arbitrary", "arbitrary"),
        name="outproj",
    )(merged, x, w_out, mod, g2.reshape(1, d))


def _ffn_kernel(h2_ref, x1_ref, mod_ref, wa_ref, wb_ref, wd_ref, wc_ref, bc_ref, prev_ref, fg_ref,
                y_ref, st_ref, abuf, carry, acc_sc, *, tm, nf):
    m = pl.program_id(1)
    f = pl.program_id(2)

    @pl.when(f == 0)
    def _():
        acc_sc[...] = jnp.zeros(acc_sc.shape, F32)

    h2 = h2_ref[0]
    prev = jnp.where(m == 0, prev_ref[0], carry[f])
    tf = wa_ref.shape[1]
    nsub = 2 if tf % (2 * MXU_COLS) == 0 else 1
    tfs = tf // nsub

    def up(c):
        cols = slice(c * tfs, (c + 1) * tfs)
        return (jnp.dot(h2, wa_ref[:, cols], preferred_element_type=F32),
                jnp.dot(h2, wb_ref[:, cols], preferred_element_type=F32))

    nxt = up(0)
    for c in range(nsub):
        cols = slice(c * tfs, (c + 1) * tfs)
        a, gate = nxt
        if c + 1 < nsub:
            nxt = up(c + 1)
        abuf[6:8, cols] = prev[:, cols]
        abuf[8:8 + tm, cols] = a
        tail = a[tm - 2:tm, :]
        carry[f, :, cols] = tail
        st_ref[0, 0, :, cols] = tail
        conv = (bc_ref[:, cols] + wc_ref[0:1, cols] * abuf[6:6 + tm, cols]
                + wc_ref[1:2, cols] * abuf[7:7 + tm, cols] + wc_ref[2:3, cols] * a)
        u = (conv * jax.nn.sigmoid(conv) * gate).astype(BF16)
        acc_sc[...] += jnp.dot(u, wd_ref[cols, :], preferred_element_type=F32)

    @pl.when(f == nf - 1)
    def _():
        x2 = x1_ref[0] + mod_ref[0, 5:6, :] * acc_sc[...]
        y_ref[0] = x2 * lax.rsqrt(jnp.mean(x2 * x2, axis=-1, keepdims=True) + EPS) * fg_ref[...]


def _ffn(h2, x1, mod, wa, wb, wd, wconv, bconv, prev, final_g):
    b, t, d = x1.shape
    dff = wa.shape[1]
    tm = _pick(t, 512, 16)
    tf = _pick(dff, 512, LANES)
    nf = dff // tf
    row = lambda i, j, k: (i, j, 0)
    kernel = functools.partial(_ffn_kernel, tm=tm, nf=nf)
    return pl.pallas_call(
        kernel,
        grid=(b, t // tm, nf),
        in_specs=[pl.BlockSpec((1, tm, d), row),
                  pl.BlockSpec((1, tm, d), row),
                  pl.BlockSpec((1, 6, d), lambda i, j, k: (i, 0, 0)),
                  pl.BlockSpec((d, tf), lambda i, j, k: (0, k)),
                  pl.BlockSpec((d, tf), lambda i, j, k: (0, k)),
                  pl.BlockSpec((tf, d), lambda i, j, k: (k, 0)),
                  pl.BlockSpec((CONV_W, tf), lambda i, j, k: (0, k)),
                  pl.BlockSpec((1, tf), lambda i, j, k: (0, k)),
                  pl.BlockSpec((1, CONV_W - 1, tf), lambda i, j, k: (i, 0, k)),
                  pl.BlockSpec((1, d), lambda i, j, k: (0, 0))],
        out_specs=[pl.BlockSpec((1, tm, d), row),
                   pl.BlockSpec((1, 1, CONV_W - 1, tf), lambda i, j, k: (i, j, 0, k))],
        out_shape=[jax.ShapeDtypeStruct((b, t, d), F32),
                   jax.ShapeDtypeStruct((b, t // tm, CONV_W - 1, dff), F32)],
        scratch_shapes=[pltpu.VMEM((tm + 8, tf), F32),
                        pltpu.VMEM((nf, CONV_W - 1, tf), F32),
                        pltpu.VMEM((tm, d), F32)],
        compiler_params=_cp("arbitrary", "arbitrary", "arbitrary"),
        name="ffn",
    )(h2, x1, mod, wa, wb, wd, wconv, bconv.reshape(1, dff), prev, final_g.reshape(1, d))


def _prep_weights(w_in, w_gate, w_proj_a, w_proj_b, w_out, w_up, w_down):
    o = np.cumsum((0, W_A, W_A, W_A, N_IDX_HEADS * IDX_DIM, IDX_DIM, N_IDX_HEADS, W_B, W_B, W_B))
    col = lambda a, b: w_in[:, int(o[a]):int(o[b])]
    d = w_in.shape[0]
    dff = w_down.shape[0]
    pad = jnp.zeros((d, LANES - IDX_DIM - N_IDX_HEADS), w_in.dtype)
    qi_hi, qi_lo = _split_bf16(col(3, 4))
    kiwi_hi, kiwi_lo = _split_bf16(jnp.concatenate([col(4, 6), pad], axis=1))
    return dict(
        q2=jnp.concatenate([col(0, 1), col(6, 7)], axis=1).astype(BF16),
        ka=col(1, 2).astype(BF16),
        va=col(2, 3).astype(BF16),
        qi_hi=qi_hi, qi_lo=qi_lo, kiwi_hi=kiwi_hi, kiwi_lo=kiwi_lo,
        kvb=col(7, 9).astype(BF16),
        gate=w_gate.astype(BF16),
        pa=w_proj_a.astype(BF16),
        pb=w_proj_b.astype(BF16),
        out=w_out.astype(BF16),
        up_a=w_up[:, :dff].astype(BF16),
        up_b=w_up[:, dff:].astype(BF16),
        down=w_down.astype(BF16),
    )


def _trunk_layer(x, mod, cache, wts, norm1_g, rel_bias, b_gate, norm2_g, w_conv, b_conv, final_g):
    b, t, d = x.shape
    dff = wts["down"].shape[0]
    h, h_lo = _norm_mod(x, norm1_g, mod)
    h2d = h.reshape(b * t, d)
    hl2d = h_lo.reshape(b * t, d)
    (q2,) = _mm(h2d, wts["q2"], (BF16,))
    ka32, ka16 = _mm(h2d, wts["ka"], (F32, BF16))
    va32, va16 = _mm(h2d, wts["va"], (F32, BF16))
    qi = _mm_split(h2d, hl2d, wts["qi_hi"], wts["qi_lo"]).reshape(b, t, N_IDX_HEADS * IDX_DIM)
    kiwi = _mm_split(h2d, hl2d, wts["kiwi_hi"], wts["kiwi_lo"])
    (kvb16,) = _mm(h2d, wts["kvb"], (BF16,))
    q2 = q2.reshape(b, t, W_A + W_B)
    kiwi = kiwi.reshape(b, t, LANES)
    ki = kiwi[:, :, :IDX_DIM]
    wi = kiwi[:, :, IDX_DIM:IDX_DIM + N_IDX_HEADS]
    ka16 = ka16.reshape(b, t, W_A)
    va16 = va16.reshape(b, t, W_A)
    kvb16 = kvb16.reshape(b, t, 2 * W_B)

    if cache is None:
        rows_tail = min(BAND_PAST, t)
        tq_a = _pick(t, 256, CHUNK)
        tk_a = _pick(t, 512, LANES)
        o_a = _mixer_a(q2, qi, wi, ki, ka16, va16, past=0, valid_len=t, topk=min(TOPK_MAX, t // 4),
                       tq=tq_a, tk=tk_a)
        tq_b = _pick(t, 256, CHUNK)
        assert BAND_PAST % tq_b == 0
        back = BAND_PAST // tq_b
        lows = [BAND_PAST - v * tq_b for v in range(back + 1)]
        bias = _band_bias(rel_bias, tq_b, BAND_PAST + tq_b, BAND_PAST, lows, BAND_PAST + tq_b)
        o_b = _mixer_b(q2, kvb16, bias, tq=tq_b, tkb=tq_b, nkb=back + 1, back=back)
        conv_prev = jnp.zeros((b, CONV_W - 1, dff), F32)
    else:
        ck, cv, cki, cbk, cbv, conv_prev = cache
        past = ck.shape[1]
        rows_tail = t
        tk_a = LANES
        l_valid = past + t
        lp = -(-l_valid // tk_a) * tk_a
        padk = jnp.zeros((b, lp - l_valid, W_A), BF16)
        k_all = jnp.concatenate([ck.reshape(b, past, W_A).astype(BF16), ka16, padk], axis=1)
        v_all = jnp.concatenate([cv.reshape(b, past, W_A).astype(BF16), va16, padk], axis=1)
        ki_all = jnp.concatenate([cki, ki, jnp.zeros((b, lp - l_valid, IDX_DIM), F32)], axis=1)
        tq_a = -(-t // LANES) * LANES
        padq = lambda a: jnp.pad(a, ((0, 0), (0, tq_a - t), (0, 0)))
        o_a = _mixer_a(padq(q2), padq(qi), padq(wi), ki_all, k_all, v_all, past=past, valid_len=l_valid,
                       topk=min(TOPK_MAX, l_valid // 4), tq=tq_a, tk=tk_a)[:, :t]
        rows = cbk.shape[1]
        assert past % CHUNK == 0 and rows % CHUNK == 0
        lb_valid = rows + t
        lb = -(-lb_valid // LANES) * LANES
        kv_cache = jnp.concatenate([cbk.reshape(b, rows, W_B), cbv.reshape(b, rows, W_B)], axis=2).astype(BF16)
        kv_all = jnp.concatenate([kv_cache, kvb16, jnp.zeros((b, lb - lb_valid, 2 * W_B), BF16)], axis=1)
        bias = _band_bias(rel_bias, t, lb, rows, [0], lb_valid)
        o_b = _mixer_b(q2, kv_all, bias, tq=t, tkb=lb, nkb=1, back=0)

    (tail32,) = _mm(h[:, t - rows_tail:].reshape(b * rows_tail, d), wts["kvb"], (F32,))
    tail32 = tail32.reshape(b, rows_tail, 2, N_HEADS_B, HEAD_DIM)
    new_bk, new_bv = tail32[:, :, 0], tail32[:, :, 1]

    merged = _merge(h2d, o_a.reshape(b * t, W_A), o_b.reshape(b * t, W_B), wts["pa"], wts["pb"],
                    wts["gate"], b_gate)
    x1, h2 = _outproj(merged.reshape(b, t, d), x, wts["out"], mod, norm2_g)
    y, tails = _ffn(h2, x1, mod, wts["up_a"], wts["up_b"], wts["down"], w_conv, b_conv,
                    conv_prev, final_g)
    conv_state = tails[:, -1]
    state = (ka32.reshape(b, t, N_HEADS_A, HEAD_DIM), va32.reshape(b, t, N_HEADS_A, HEAD_DIM), ki,
             new_bk, new_bv, conv_state)
    return y, state


def kernel(x_prompt, x_sample, cache_a_k, cache_a_v, cache_idx_k, cache_b_k, cache_b_v, state_ffn_conv,
           c_prompt, c_sample, w_ada, b_ada, norm1_g, w_in, rel_bias, w_gate, b_gate, w_proj_a, w_proj_b,
           w_out, norm2_g, w_up, w_conv, b_conv, w_down, final_g):
    depth = w_ada.shape[0]
    assert depth == 1, "the fused final RMSNorm assumes a single layer"
    d = x_prompt.shape[-1]
    nb_p = x_prompt.shape[0]
    c_all = jnp.concatenate([c_prompt, c_sample], axis=0)
    xp, xs = x_prompt, x_sample
    states_p, states_s = [], []
    for l in range(depth):
        wts = _prep_weights(w_in[l], w_gate[l], w_proj_a[l], w_proj_b[l], w_out[l], w_up[l], w_down[l])
        mod = _ada(c_all, w_ada[l], b_ada[l]).reshape(c_all.shape[0], 6, d)
        args = (wts, norm1_g[l], rel_bias[l], b_gate[l], norm2_g[l], w_conv[l], b_conv[l], final_g)
        xp, st_p = _trunk_layer(xp, mod[:nb_p], None, *args)
        states_p.append(st_p)
        cache_l = (cache_a_k[l], cache_a_v[l], cache_idx_k[l], cache_b_k[l], cache_b_v[l], state_ffn_conv[l])
        xs, st_s = _trunk_layer(xs, mod[nb_p:], cache_l, *args)
        states_s.append(st_s)
    sp = [t[0][None] for t in zip(*states_p)]
    ss = [t[0][None] for t in zip(*states_s)]
    return (xp, xs, *sp, *ss)
```

```python
import functools

import jax
import jax.numpy as jnp
import numpy as np
from jax import lax
from jax.experimental import pallas as pl
from jax.experimental.pallas import tpu as pltpu

F32 = jnp.float32
BF16 = jnp.bfloat16

CHUNK = 64
CHUNK_SHIFT = 6
HEAD_DIM = 128
N_HEADS_A = 8
N_IDX_HEADS = 16
IDX_DIM = 64
TOPK_MAX = 256
N_HEADS_B = 8
BAND_CHUNKS = 8
BAND_PAST = BAND_CHUNKS * CHUNK
REL_CLIP = 2 * CHUNK
CONV_W = 3
EPS = 1e-6
NEG = -1e30
W_A = N_HEADS_A * HEAD_DIM
W_B = N_HEADS_B * HEAD_DIM
ATTN_SCALE = HEAD_DIM ** -0.5
LOG2E = 1.4426950408889634
HALF_BITS = 16
HALF_MASK = 0xFFFF
HALF_BIAS = 1 << (HALF_BITS - 1)
LANES = 128
MXU_COLS = 256
VMEM_LIMIT = 56 * 1024 * 1024
FFN_ACC_BYTES = 8 * 1024 * 1024


def _cp(*sem):
    return pltpu.CompilerParams(dimension_semantics=sem, vmem_limit_bytes=VMEM_LIMIT)


def _pick(dim, pref, align):
    t = min(pref, dim)
    t -= t % align
    while t >= align:
        if dim % t == 0:
            return t
        t -= align
    return dim


def _ada_kernel(c_ref, w_ref, b_ref, o_ref):
    c = c_ref[...]
    s = c * jax.nn.sigmoid(c)
    o_ref[...] = jnp.dot(s, w_ref[...], preferred_element_type=F32,
                         precision=lax.Precision.HIGHEST) + b_ref[...]


def _ada(c, w, b):
    r, d = c.shape
    n = w.shape[1]
    tn = _pick(n, 1024, LANES)
    return pl.pallas_call(
        _ada_kernel,
        grid=(n // tn,),
        in_specs=[pl.BlockSpec((r, d), lambda j: (0, 0)),
                  pl.BlockSpec((d, tn), lambda j: (0, j)),
                  pl.BlockSpec((1, tn), lambda j: (0, j))],
        out_specs=pl.BlockSpec((r, tn), lambda j: (0, j)),
        out_shape=jax.ShapeDtypeStruct((r, n), F32),
        compiler_params=_cp("arbitrary"),
        name="ada",
    )(c, w, b.reshape(1, n))


def _rms_mod(x, g, shift, scale):
    y = x * lax.rsqrt(jnp.mean(x * x, axis=-1, keepdims=True) + EPS) * g
    return y * (1.0 + scale) + shift


def _split_bf16(x):
    hi = x.astype(BF16)
    return hi, (x - hi.astype(F32)).astype(BF16)


def _norm_mod_kernel(x_ref, g_ref, mod_ref, hi_ref, lo_ref):
    h = _rms_mod(x_ref[0], g_ref[...], mod_ref[0, 0:1, :], mod_ref[0, 1:2, :])
    hi, lo = _split_bf16(h)
    hi_ref[0] = hi
    lo_ref[0] = lo


def _norm_mod(x, g, mod):
    b, t, d = x.shape
    tm = _pick(t, 512, 16)
    row = lambda i, j: (i, j, 0)
    return pl.pallas_call(
        _norm_mod_kernel,
        grid=(b, t // tm),
        in_specs=[pl.BlockSpec((1, tm, d), row),
                  pl.BlockSpec((1, d), lambda i, j: (0, 0)),
                  pl.BlockSpec((1, 6, d), lambda i, j: (i, 0, 0))],
        out_specs=[pl.BlockSpec((1, tm, d), row), pl.BlockSpec((1, tm, d), row)],
        out_shape=[jax.ShapeDtypeStruct((b, t, d), BF16), jax.ShapeDtypeStruct((b, t, d), BF16)],
        compiler_params=_cp("arbitrary", "arbitrary"),
        name="norm_mod",
    )(x, g.reshape(1, d), mod)


def _mm_kernel(a_ref, w_ref, *o_refs):
    acc = jnp.dot(a_ref[...], w_ref[...], preferred_element_type=F32)
    for o_ref in o_refs:
        o_ref[...] = acc.astype(o_ref.dtype)


def _mm(a, w, dtypes):
    m, k = a.shape
    n = w.shape[1]
    tm = _pick(m, 1024, 16)
    tn = _pick(n, 1024, LANES)
    outs = pl.pallas_call(
        _mm_kernel,
        grid=(m // tm, n // tn),
        in_specs=[pl.BlockSpec((tm, k), lambda i, j: (i, 0)),
                  pl.BlockSpec((k, tn), lambda i, j: (0, j))],
        out_specs=[pl.BlockSpec((tm, tn), lambda i, j: (i, j)) for _ in dtypes],
        out_shape=[jax.ShapeDtypeStruct((m, n), dt) for dt in dtypes],
        compiler_params=_cp("arbitrary", "arbitrary"),
        name="proj",
    )(a, w)
    return outs


def _mm_split_kernel(ah_ref, al_ref, wh_ref, wl_ref, o_ref):
    ah = ah_ref[...]
    acc = jnp.dot(ah, wh_ref[...], preferred_element_type=F32)
    acc = acc + jnp.dot(al_ref[...], wh_ref[...], preferred_element_type=F32)
    acc = acc + jnp.dot(ah, wl_ref[...], preferred_element_type=F32)
    o_ref[...] = acc


def _mm_split(a_hi, a_lo, w_hi, w_lo):
    m, k = a_hi.shape
    n = w_hi.shape[1]
    tm = _pick(m, 1024, 16)
    tn = _pick(n, 1024, LANES)
    a_spec = pl.BlockSpec((tm, k), lambda i, j: (i, 0))
    w_spec = pl.BlockSpec((k, tn), lambda i, j: (0, j))
    return pl.pallas_call(
        _mm_split_kernel,
        grid=(m // tm, n // tn),
        in_specs=[a_spec, a_spec, w_spec, w_spec],
        out_specs=pl.BlockSpec((tm, tn), lambda i, j: (i, j)),
        out_shape=jax.ShapeDtypeStruct((m, n), F32),
        compiler_params=_cp("arbitrary", "arbitrary"),
        name="proj_split",
    )(a_hi, a_lo, w_hi, w_lo)


def _mixer_a_kernel(tile_ref, block_ref, qa_ref, qi_ref, wit_ref, ki2_ref, k_ref, vt_ref, o_ref,
                    key_sc, hi_sc, lo_sc, thr_sc, m_sc, l_sc, acc_sc, qx_sc,
                    *, tq, tk, past, valid_len, topk):
    i = tile_ref[pl.program_id(1)]
    j = block_ref[pl.program_id(1)]
    q0 = past + i * tq
    kend = jnp.minimum(((q0 + tq + CHUNK - 1) // CHUNK) * CHUNK, valid_len)
    nblk = (kend + tk - 1) // tk
    slab16 = 16

    def admissible(jb):
        kpos = jb * tk + lax.broadcasted_iota(jnp.int32, (tk, tq), 0)
        qpos = q0 + lax.broadcasted_iota(jnp.int32, (tk, tq), 1)
        adm = (jnp.right_shift(kpos, CHUNK_SHIFT) <= jnp.right_shift(qpos, CHUNK_SHIFT)) & (kpos < valid_len)
        return adm, qpos, kpos

    @pl.when(j == 0)
    def _index_and_threshold():
        qt = qi_ref[0].T
        q_hi, q_lo = _split_bf16(qt)
        for h in range(N_IDX_HEADS):
            rows = slice(h * IDX_DIM, (h + 1) * IDX_DIM)
            qx_sc[h, 0 * IDX_DIM:1 * IDX_DIM, :] = q_hi[rows]
            qx_sc[h, 1 * IDX_DIM:2 * IDX_DIM, :] = q_hi[rows]
            qx_sc[h, 2 * IDX_DIM:3 * IDX_DIM, :] = q_lo[rows]
            qx_sc[h, 3 * IDX_DIM:4 * IDX_DIM, :] = q_lo[rows]

        first = lax.broadcasted_iota(jnp.int32, (tk, LANES), 1) < IDX_DIM

        def score_block(jb, carry):
            dup = ki2_ref[0, jb]
            kk = jnp.where(first, dup, dup - dup.astype(BF16).astype(F32)).astype(BF16)
            kx = jnp.concatenate([kk, kk], axis=1)
            acc = jnp.zeros((tk, tq), F32)
            for h in range(N_IDX_HEADS):
                rel = jnp.dot(kx, qx_sc[h], preferred_element_type=F32)
                acc = acc + wit_ref[0, h:h + 1, :] * jnp.maximum(rel, 0.0)
            adm, _, _ = admissible(jb)
            acc = jnp.where(adm, acc, NEG)
            bits = pltpu.bitcast(acc, jnp.int32)
            key = bits ^ (jnp.right_shift(bits, 31) & 0x7FFFFFFF)
            key_sc[jb] = key
            hi_sc[jb] = jnp.right_shift(key, HALF_BITS).astype(jnp.int16)
            lo_sc[jb] = ((key & HALF_MASK) - HALF_BIAS).astype(jnp.int16)
            return carry

        lax.fori_loop(0, nblk, score_block, 0)

        def count_ge(src_sc, cand):
            cand16 = cand.astype(jnp.int16)

            def count_block(jb, cnt):
                for r in range(tk // slab16):
                    rows = slice(r * slab16, (r + 1) * slab16)
                    cnt = cnt + jnp.where(src_sc[jb, rows, :] >= cand16, jnp.int16(1), jnp.int16(0))
                return cnt

            cnt = lax.fori_loop(0, nblk, count_block, jnp.zeros((slab16, tq), jnp.int16))
            return jnp.sum(cnt.astype(jnp.int32), axis=0, keepdims=True)

        def kth_largest16(src_sc, want):
            def bit_step(b, t_u):
                cand_u = t_u | lax.shift_left(jnp.int32(1), HALF_BITS - 1 - b)
                return jnp.where(count_ge(src_sc, cand_u - HALF_BIAS) >= want, cand_u, t_u)
            t_u = lax.fori_loop(0, HALF_BITS, bit_step, jnp.zeros((slab16, tq), jnp.int32))
            return t_u - HALF_BIAS

        t_hi = kth_largest16(hi_sc, topk)
        above = jnp.where(t_hi == HALF_BIAS - 1, 0, count_ge(hi_sc, jnp.minimum(t_hi + 1, HALF_BIAS - 1)))
        t_hi16 = t_hi.astype(jnp.int16)

        def keep_group(jb, carry):
            for r in range(tk // slab16):
                rows = slice(r * slab16, (r + 1) * slab16)
                lo_sc[jb, rows, :] = jnp.where(hi_sc[jb, rows, :] == t_hi16, lo_sc[jb, rows, :],
                                               jnp.int16(-HALF_BIAS))
            return carry

        lax.fori_loop(0, nblk, keep_group, 0)
        t_lo = kth_largest16(lo_sc, topk - above)
        thr_sc[...] = (t_hi * (2 * HALF_BIAS) + (t_lo + HALF_BIAS))[0:8, :]
        m_sc[...] = jnp.full(m_sc.shape, -jnp.inf, F32)
        l_sc[...] = jnp.zeros(l_sc.shape, F32)
        acc_sc[...] = jnp.zeros(acc_sc.shape, F32)

    def _attend():
        adm, qpos, kpos = admissible(j)
        sel = (key_sc[j] >= thr_sc[0:1, :]) & adm
        dist = jnp.where(sel, jnp.abs(qpos - kpos).astype(F32), -NEG)

        nch = 2 if tk % (2 * MXU_COLS) == 0 else 1
        ck = tk // nch

        def logits(h, c):
            hs = slice(h * HEAD_DIM, (h + 1) * HEAD_DIM)
            return lax.dot_general(k_ref[0, c * ck:(c + 1) * ck, hs], qa_ref[0, :, hs],
                                   (((1,), (1,)), ((), ())), preferred_element_type=F32)

        s_next = [logits(0, c) for c in range(nch)]
        for h in range(N_HEADS_A):
            hs = slice(h * HEAD_DIM, (h + 1) * HEAD_DIM)
            s_raw, s_next = s_next, []
            m_prev = m_sc[h:h + 1, :]
            m_new = m_prev
            s = []
            for c in range(nch):
                if h + 1 < N_HEADS_A:
                    s_next.append(logits(h + 1, c))
                s.append(s_raw[c] * (ATTN_SCALE * LOG2E)
                         - ((2.0 ** -(h + 1)) * LOG2E) * dist[c * ck:(c + 1) * ck])
                m_new = jnp.maximum(m_new, jnp.max(s[c], axis=0, keepdims=True))
            alpha = jnp.exp2(m_prev - m_new)
            l_new = alpha * l_sc[h:h + 1, :]
            acc = alpha * acc_sc[hs, :]
            for c in range(nch):
                p = jnp.exp2(s[c] - m_new)
                l_new = l_new + jnp.sum(p, axis=0, keepdims=True)
                acc = acc + jnp.dot(vt_ref[0, hs, c * ck:(c + 1) * ck], p.astype(BF16),
                                    preferred_element_type=F32)
            l_sc[h:h + 1, :] = l_new
            acc_sc[hs, :] = acc
            m_sc[h:h + 1, :] = m_new

    _attend()

    @pl.when(j == nblk - 1)
    def _finalize():
        for h in range(N_HEADS_A):
            hs = slice(h * HEAD_DIM, (h + 1) * HEAD_DIM)
            o_ref[0, :, hs] = (acc_sc[hs, :] / l_sc[h:h + 1, :]).T.astype(o_ref.dtype)


def _mixer_a(q2, qi, wi, ki_all, k_all, v_all, *, past, valid_len, topk, tq, tk):
    b, t, _ = q2.shape
    lp = k_all.shape[1]
    nkb = lp // tk
    assert lp % tk == 0 and t % tq == 0 and tk % LANES == 0 and past % CHUNK == 0
    ki2 = jnp.concatenate([ki_all, ki_all], axis=2).reshape(b, nkb, tk, 2 * IDX_DIM)
    wit = jnp.swapaxes(wi, 1, 2)
    vt = jnp.swapaxes(v_all, 1, 2)

    def blocks_needed(i):
        kend = min(-(-(past + (i + 1) * tq) // CHUNK) * CHUNK, valid_len)
        return -(-kend // tk)

    pairs = [(i, j) for i in range(t // tq) for j in range(blocks_needed(i))]
    tile_of = jnp.asarray([p[0] for p in pairs], jnp.int32)
    block_of = jnp.asarray([p[1] for p in pairs], jnp.int32)
    q_map = lambda bb, p, ti, bj: (bb, ti[p], 0)
    kernel = functools.partial(_mixer_a_kernel, tq=tq, tk=tk, past=past, valid_len=valid_len, topk=topk)
    grid_spec = pltpu.PrefetchScalarGridSpec(
        num_scalar_prefetch=2,
        grid=(b, len(pairs)),
        in_specs=[pl.BlockSpec((1, tq, W_A), q_map),
                  pl.BlockSpec((1, tq, N_IDX_HEADS * IDX_DIM), q_map),
                  pl.BlockSpec((1, N_IDX_HEADS, tq), lambda bb, p, ti, bj: (bb, 0, ti[p])),
                  pl.BlockSpec((1, nkb, tk, 2 * IDX_DIM), lambda bb, p, ti, bj: (bb, 0, 0, 0)),
                  pl.BlockSpec((1, tk, W_A), lambda bb, p, ti, bj: (bb, bj[p], 0)),
                  pl.BlockSpec((1, W_A, tk), lambda bb, p, ti, bj: (bb, 0, bj[p]))],
        out_specs=pl.BlockSpec((1, tq, W_A), q_map),
        scratch_shapes=[pltpu.VMEM((nkb, tk, tq), jnp.int32),
                        pltpu.VMEM((nkb, tk, tq), jnp.int16),
                        pltpu.VMEM((nkb, tk, tq), jnp.int16),
                        pltpu.VMEM((8, tq), jnp.int32),
                        pltpu.VMEM((N_HEADS_A, tq), F32),
                        pltpu.VMEM((N_HEADS_A, tq), F32),
                        pltpu.VMEM((W_A, tq), F32),
                        pltpu.VMEM((N_IDX_HEADS, 4 * IDX_DIM, tq), BF16)])
    return pl.pallas_call(
        kernel,
        grid_spec=grid_spec,
        out_shape=jax.ShapeDtypeStruct((b, t, W_A), BF16),
        compiler_params=_cp("arbitrary", "arbitrary"),
        name="mixer_a",
    )(tile_of, block_of, q2, qi, wit, ki2, k_all, vt)


def _mixer_b_kernel(*refs, nkb, tkb):
    q_ref = refs[0]
    k_refs = refs[1:1 + nkb]
    v_refs = refs[1 + nkb:1 + 2 * nkb]
    bias_ref = refs[1 + 2 * nkb]
    o_ref = refs[2 + 2 * nkb]
    for h in range(N_HEADS_B):
        hs = slice(h * HEAD_DIM, (h + 1) * HEAD_DIM)
        q = q_ref[0, :, hs]
        parts = [lax.dot_general(q, kr[0, :, hs], (((1,), (1,)), ((), ())), preferred_element_type=F32)
                 for kr in k_refs]
        s = parts[0] if nkb == 1 else jnp.concatenate(parts, axis=1)
        s = s * ATTN_SCALE + bias_ref[0, h]
        p = jnp.exp(s - jnp.max(s, axis=1, keepdims=True))
        l = jnp.sum(p, axis=1, keepdims=True)
        pb = p.astype(BF16)
        pv = None
        for c, vr in enumerate(v_refs):
            t = jnp.dot(pb[:, c * tkb:(c + 1) * tkb], vr[0, :, hs], preferred_element_type=F32)
            pv = t if pv is None else pv + t
        o_ref[0, :, hs] = (pv / l).astype(o_ref.dtype)


def _band_bias(rel_bias, tq, wk, off, lows, hi):
    nh = rel_bias.shape[0]
    n = wk + tq
    e = np.concatenate([np.arange(wk + 1), np.arange(-(tq - 1), 0)])
    f = rel_bias[:, np.clip(off - e, -REL_CLIP, REL_CLIP) + REL_CLIP].astype(F32)
    bias = jnp.tile(f, (1, tq))[:, :tq * (n - 1)].reshape(nh, tq, n - 1)[:, :, :wk]
    r = np.arange(tq)[:, None]
    c = np.arange(wk)[None, :]
    dq = r // CHUNK
    dk = np.floor_divide(c - off, CHUNK)
    ok = (dk <= dq) & (dk >= dq - BAND_CHUNKS) & (c < hi)
    tiles = [jnp.where((ok & (c >= lo))[None], bias, NEG) for lo in lows]
    return jnp.stack(tiles)


def _mixer_b(q2, kv, bias, *, tq, tkb, nkb, back):
    b, t, _ = q2.shape
    nvar = bias.shape[0]
    wk = nkb * tkb
    kmaps = [functools.partial(lambda bb, i, p, col: (bb, jnp.maximum(i - back + p, 0), col), p=p, col=0)
             for p in range(nkb)]
    vmaps = [functools.partial(lambda bb, i, p, col: (bb, jnp.maximum(i - back + p, 0), col), p=p, col=1)
             for p in range(nkb)]
    kernel = functools.partial(_mixer_b_kernel, nkb=nkb, tkb=tkb)
    return pl.pallas_call(
        kernel,
        grid=(b, t // tq),
        in_specs=([pl.BlockSpec((1, tq, W_B), lambda bb, i: (bb, i, 1))]
                  + [pl.BlockSpec((1, tkb, W_B), m) for m in kmaps]
                  + [pl.BlockSpec((1, tkb, W_B), m) for m in vmaps]
                  + [pl.BlockSpec((1, N_HEADS_B, tq, wk), lambda bb, i: (jnp.minimum(i, nvar - 1), 0, 0, 0))]),
        out_specs=pl.BlockSpec((1, tq, W_B), lambda bb, i: (bb, i, 0)),
        out_shape=jax.ShapeDtypeStruct((b, t, W_B), BF16),
        compiler_params=_cp("arbitrary", "arbitrary"),
        name="mixer_b",
    )(q2, *([kv] * (2 * nkb)), bias)


def _merge_kernel(h_ref, oa_ref, ob_ref, pa_ref, pb_ref, wga_ref, wgb_ref, bga_ref, bgb_ref, o_ref):
    h = h_ref[...]
    ga = jax.nn.sigmoid(jnp.dot(h, wga_ref[...], preferred_element_type=F32) + bga_ref[...])
    gb = jax.nn.sigmoid(jnp.dot(h, wgb_ref[...], preferred_element_type=F32) + bgb_ref[...])
    ya = jnp.dot(oa_ref[...], pa_ref[...], preferred_element_type=F32)
    yb = jnp.dot(ob_ref[...], pb_ref[...], preferred_element_type=F32)
    o_ref[...] = (ga * ya + gb * yb).astype(o_ref.dtype)


def _merge(h, oa, ob, pa, pb, wg, bg):
    m, d = h.shape
    tm = _pick(m, 512, 16)
    tn = _pick(d, 512, LANES)
    nb = d // tn
    bg2 = bg.reshape(1, 2 * d)
    return pl.pallas_call(
        _merge_kernel,
        grid=(m // tm, nb),
        in_specs=[pl.BlockSpec((tm, d), lambda i, j: (i, 0)),
                  pl.BlockSpec((tm, W_A), lambda i, j: (i, 0)),
                  pl.BlockSpec((tm, W_B), lambda i, j: (i, 0)),
                  pl.BlockSpec((W_A, tn), lambda i, j: (0, j)),
                  pl.BlockSpec((W_B, tn), lambda i, j: (0, j)),
                  pl.BlockSpec((d, tn), lambda i, j: (0, j)),
                  pl.BlockSpec((d, tn), lambda i, j: (0, j + nb)),
                  pl.BlockSpec((1, tn), lambda i, j: (0, j)),
                  pl.BlockSpec((1, tn), lambda i, j: (0, j + nb))],
        out_specs=pl.BlockSpec((tm, tn), lambda i, j: (i, j)),
        out_shape=jax.ShapeDtypeStruct((m, d), BF16),
        compiler_params=_cp("arbitrary", "arbitrary"),
        name="merge",
    )(h, oa, ob, pa, pb, wg, wg, bg2, bg2)


def _outproj_kernel(mg_ref, x_ref, w_ref, mod_ref, g_ref, x1_ref, h2_ref):
    o = jnp.dot(mg_ref[0], w_ref[...], preferred_element_type=F32)
    x1 = x_ref[0] + mod_ref[0, 2:3, :] * o
    x1_ref[0] = x1
    h2_ref[0] = _rms_mod(x1, g_ref[...], mod_ref[0, 3:4, :], mod_ref[0, 4:5, :]).astype(h2_ref.dtype)


def _outproj(merged, x, w_out, mod, g2):
    b, t, d = x.shape
    tm = _pick(t, 256, 16)
    row = lambda i, j: (i, j, 0)
    return pl.pallas_call(
        _outproj_kernel,
        grid=(b, t // tm),
        in_specs=[pl.BlockSpec((1, tm, d), row),
                  pl.BlockSpec((1, tm, d), row),
                  pl.BlockSpec((d, d), lambda i, j: (0, 0)),
                  pl.BlockSpec((1, 6, d), lambda i, j: (i, 0, 0)),
                  pl.BlockSpec((1, d), lambda i, j: (0, 0))],
        out_specs=[pl.BlockSpec((1, tm, d), row), pl.BlockSpec((1, tm, d), row)],
        out_shape=[jax.ShapeDtypeStruct((b, t, d), F32), jax.ShapeDtypeStruct((b, t, d), BF16)],
        compiler_params=_cp("arbitrary", "arbitrary"),
        name="outproj",
    )(merged, x, w_out, mod, g2.reshape(1, d))


def _ffn_kernel(h2_ref, x1_ref, mod_ref, wa_ref, wb_ref, wd_ref, wc_ref, bc_ref, prev_ref, fg_ref,
                y_ref, st_ref, abuf, carry, acc_sc, *, tm, nf, weights_outer):
    if weights_outer:
        f, slot, m = pl.program_id(0), pl.program_id(1), pl.program_id(2)
    else:
        slot, m, f = 0, pl.program_id(1), pl.program_id(2)

    @pl.when(f == 0)
    def _():
        acc_sc[slot] = jnp.zeros(acc_sc.shape[1:], F32)

    h2 = h2_ref[0]
    prev = jnp.where(m == 0, prev_ref[0], carry[f])
    tf = wa_ref.shape[1]
    nsub = 2 if tf % (2 * MXU_COLS) == 0 else 1
    tfs = tf // nsub

    def up(c):
        cols = slice(c * tfs, (c + 1) * tfs)
        return (jnp.dot(h2, wa_ref[:, cols], preferred_element_type=F32),
                jnp.dot(h2, wb_ref[:, cols], preferred_element_type=F32))

    nxt = up(0)
    for c in range(nsub):
        cols = slice(c * tfs, (c + 1) * tfs)
        a, gate = nxt
        if c + 1 < nsub:
            nxt = up(c + 1)
        abuf[6:8, cols] = prev[:, cols]
        abuf[8:8 + tm, cols] = a
        tail = a[tm - 2:tm, :]
        carry[f, :, cols] = tail
        st_ref[0, 0, :, cols] = tail
        conv = (bc_ref[:, cols] + wc_ref[0:1, cols] * abuf[6:6 + tm, cols]
                + wc_ref[1:2, cols] * abuf[7:7 + tm, cols] + wc_ref[2:3, cols] * a)
        u = (conv * jax.nn.sigmoid(conv) * gate).astype(BF16)
        acc_sc[slot] += jnp.dot(u, wd_ref[cols, :], preferred_element_type=F32)

    @pl.when(f == nf - 1)
    def _():
        x2 = x1_ref[0] + mod_ref[0, 5:6, :] * acc_sc[slot]
        y_ref[0] = x2 * lax.rsqrt(jnp.mean(x2 * x2, axis=-1, keepdims=True) + EPS) * fg_ref[...]


def _ffn(h2, x1, mod, wa, wb, wd, wconv, bconv, prev, final_g):
    b, t, d = x1.shape
    dff = wa.shape[1]
    tm = _pick(t, 512, 16)
    tf = _pick(dff, 512, LANES)
    nf = dff // tf
    nm = t // tm
    weights_outer = nm == 1 and b * tm * d * 4 <= FFN_ACC_BYTES
    if weights_outer:
        grid = (nf, b, nm)
        spec = lambda shape, fn: pl.BlockSpec(shape, lambda k, i, j: fn(i, j, k))
        row = lambda i, j, k: (jnp.where(k == nf - 1, i, 0), jnp.where(k == nf - 1, j, 0), 0)
    else:
        grid = (b, nm, nf)
        spec = lambda shape, fn: pl.BlockSpec(shape, fn)
        row = lambda i, j, k: (i, j, 0)
    kernel = functools.partial(_ffn_kernel, tm=tm, nf=nf, weights_outer=weights_outer)
    return pl.pallas_call(
        kernel,
        grid=grid,
        in_specs=[spec((1, tm, d), lambda i, j, k: (i, j, 0)),
                  spec((1, tm, d), row),
                  spec((1, 6, d), lambda i, j, k: (i, 0, 0)),
                  spec((d, tf), lambda i, j, k: (0, k)),
                  spec((d, tf), lambda i, j, k: (0, k)),
                  spec((tf, d), lambda i, j, k: (k, 0)),
                  spec((CONV_W, tf), lambda i, j, k: (0, k)),
                  spec((1, tf), lambda i, j, k: (0, k)),
                  spec((1, CONV_W - 1, tf), lambda i, j, k: (i, 0, k)),
                  spec((1, d), lambda i, j, k: (0, 0))],
        out_specs=[spec((1, tm, d), row),
                   spec((1, 1, CONV_W - 1, tf), lambda i, j, k: (i, j, 0, k))],
        out_shape=[jax.ShapeDtypeStruct((b, t, d), F32),
                   jax.ShapeDtypeStruct((b, nm, CONV_W - 1, dff), F32)],
        scratch_shapes=[pltpu.VMEM((tm + 8, tf), F32),
                        pltpu.VMEM((nf, CONV_W - 1, tf), F32),
                        pltpu.VMEM((b if weights_outer else 1, tm, d), F32)],
        compiler_params=_cp("arbitrary", "arbitrary", "arbitrary"),
        name="ffn",
    )(h2, x1, mod, wa, wb, wd, wconv, bconv.reshape(1, dff), prev, final_g.reshape(1, d))


def _prep_weights(w_in, w_gate, w_proj_a, w_proj_b, w_out, w_up, w_down):
    o = np.cumsum((0, W_A, W_A, W_A, N_IDX_HEADS * IDX_DIM, IDX_DIM, N_IDX_HEADS, W_B, W_B, W_B))
    col = lambda a, b: w_in[:, int(o[a]):int(o[b])]
    d = w_in.shape[0]
    dff = w_down.shape[0]
    pad = jnp.zeros((d, LANES - IDX_DIM - N_IDX_HEADS), w_in.dtype)
    qi_hi, qi_lo = _split_bf16(col(3, 4))
    kiwi_hi, kiwi_lo = _split_bf16(jnp.concatenate([col(4, 6), pad], axis=1))
    return dict(
        q2=jnp.concatenate([col(0, 1), col(6, 7)], axis=1).astype(BF16),
        ka=col(1, 2).astype(BF16),
        va=col(2, 3).astype(BF16),
        qi_hi=qi_hi, qi_lo=qi_lo, kiwi_hi=kiwi_hi, kiwi_lo=kiwi_lo,
        kvb=col(7, 9).astype(BF16),
        gate=w_gate.astype(BF16),
        pa=w_proj_a.astype(BF16),
        pb=w_proj_b.astype(BF16),
        out=w_out.astype(BF16),
        up_a=w_up[:, :dff].astype(BF16),
        up_b=w_up[:, dff:].astype(BF16),
        down=w_down.astype(BF16),
    )


def _trunk_layer(x, mod, cache, wts, norm1_g, rel_bias, b_gate, norm2_g, w_conv, b_conv, final_g):
    b, t, d = x.shape
    dff = wts["down"].shape[0]
    h, h_lo = _norm_mod(x, norm1_g, mod)
    h2d = h.reshape(b * t, d)
    hl2d = h_lo.reshape(b * t, d)
    (q2,) = _mm(h2d, wts["q2"], (BF16,))
    ka32, ka16 = _mm(h2d, wts["ka"], (F32, BF16))
    va32, va16 = _mm(h2d, wts["va"], (F32, BF16))
    qi = _mm_split(h2d, hl2d, wts["qi_hi"], wts["qi_lo"]).reshape(b, t, N_IDX_HEADS * IDX_DIM)
    kiwi = _mm_split(h2d, hl2d, wts["kiwi_hi"], wts["kiwi_lo"])
    (kvb16,) = _mm(h2d, wts["kvb"], (BF16,))
    q2 = q2.reshape(b, t, W_A + W_B)
    kiwi = kiwi.reshape(b, t, LANES)
    ki = kiwi[:, :, :IDX_DIM]
    wi = kiwi[:, :, IDX_DIM:IDX_DIM + N_IDX_HEADS]
    ka16 = ka16.reshape(b, t, W_A)
    va16 = va16.reshape(b, t, W_A)
    kvb16 = kvb16.reshape(b, t, 2 * W_B)

    if cache is None:
        rows_tail = min(BAND_PAST, t)
        tq_a = _pick(t, 256, CHUNK)
        tk_a = _pick(t, 512, LANES)
        o_a = _mixer_a(q2, qi, wi, ki, ka16, va16, past=0, valid_len=t, topk=min(TOPK_MAX, t // 4),
                       tq=tq_a, tk=tk_a)
        tq_b = _pick(t, 256, CHUNK)
        assert BAND_PAST % tq_b == 0
        back = BAND_PAST // tq_b
        lows = [BAND_PAST - v * tq_b for v in range(back + 1)]
        bias = _band_bias(rel_bias, tq_b, BAND_PAST + tq_b, BAND_PAST, lows, BAND_PAST + tq_b)
        o_b = _mixer_b(q2, kvb16, bias, tq=tq_b, tkb=tq_b, nkb=back + 1, back=back)
        conv_prev = jnp.zeros((b, CONV_W - 1, dff), F32)
    else:
        ck, cv, cki, cbk, cbv, conv_prev = cache
        past = ck.shape[1]
        rows_tail = t
        l_valid = past + t
        lp = -(-l_valid // LANES) * LANES
        tk_a = lp
        padk = jnp.zeros((b, lp - l_valid, W_A), BF16)
        k_all = jnp.concatenate([ck.reshape(b, past, W_A).astype(BF16), ka16, padk], axis=1)
        v_all = jnp.concatenate([cv.reshape(b, past, W_A).astype(BF16), va16, padk], axis=1)
        ki_all = jnp.concatenate([cki, ki, jnp.zeros((b, lp - l_valid, IDX_DIM), F32)], axis=1)
        tq_a = -(-t // LANES) * LANES
        padq = lambda a: jnp.pad(a, ((0, 0), (0, tq_a - t), (0, 0)))
        o_a = _mixer_a(padq(q2), padq(qi), padq(wi), ki_all, k_all, v_all, past=past, valid_len=l_valid,
                       topk=min(TOPK_MAX, l_valid // 4), tq=tq_a, tk=tk_a)[:, :t]
        rows = cbk.shape[1]
        assert past % CHUNK == 0 and rows % CHUNK == 0
        lb_valid = rows + t
        lb = -(-lb_valid // LANES) * LANES
        kv_cache = jnp.concatenate([cbk.reshape(b, rows, W_B), cbv.reshape(b, rows, W_B)], axis=2).astype(BF16)
        kv_all = jnp.concatenate([kv_cache, kvb16, jnp.zeros((b, lb - lb_valid, 2 * W_B), BF16)], axis=1)
        bias = _band_bias(rel_bias, t, lb, rows, [0], lb_valid)
        o_b = _mixer_b(q2, kv_all, bias, tq=t, tkb=lb, nkb=1, back=0)

    (tail32,) = _mm(h[:, t - rows_tail:].reshape(b * rows_tail, d), wts["kvb"], (F32,))
    tail32 = tail32.reshape(b, rows_tail, 2, N_HEADS_B, HEAD_DIM)
    new_bk, new_bv = tail32[:, :, 0], tail32[:, :, 1]

    merged = _merge(h2d, o_a.reshape(b * t, W_A), o_b.reshape(b * t, W_B), wts["pa"], wts["pb"],
                    wts["gate"], b_gate)
    x1, h2 = _outproj(merged.reshape(b, t, d), x, wts["out"], mod, norm2_g)
    y, tails = _ffn(h2, x1, mod, wts["up_a"], wts["up_b"], wts["down"], w_conv, b_conv,
                    conv_prev, final_g)
    conv_state = tails[:, -1]
    state = (ka32.reshape(b, t, N_HEADS_A, HEAD_DIM), va32.reshape(b, t, N_HEADS_A, HEAD_DIM), ki,
             new_bk, new_bv, conv_state)
    return y, state


def kernel(x_prompt, x_sample, cache_a_k, cache_a_v, cache_idx_k, cache_b_k, cache_b_v, state_ffn_conv,
           c_prompt, c_sample, w_ada, b_ada, norm1_g, w_in, rel_bias, w_gate, b_gate, w_proj_a, w_proj_b,
           w_out, norm2_g, w_up, w_conv, b_conv, w_down, final_g):
    depth = w_ada.shape[0]
    assert depth == 1, "the fused final RMSNorm assumes a single layer"
    d = x_prompt.shape[-1]
    nb_p = x_prompt.shape[0]
    c_all = jnp.concatenate([c_prompt, c_sample], axis=0)
    xp, xs = x_prompt, x_sample
    states_p, states_s = [], []
    for l in range(depth):
        wts = _prep_weights(w_in[l], w_gate[l], w_proj_a[l], w_proj_b[l], w_out[l], w_up[l], w_down[l])
        mod = _ada(c_all, w_ada[l], b_ada[l]).reshape(c_all.shape[0], 6, d)
        args = (wts, norm1_g[l], rel_bias[l], b_gate[l], norm2_g[l], w_conv[l], b_conv[l], final_g)
        xp, st_p = _trunk_layer(xp, mod[:nb_p], None, *args)
        states_p.append(st_p)
        cache_l = (cache_a_k[l], cache_a_v[l], cache_idx_k[l], cache_b_k[l], cache_b_v[l], state_ffn_conv[l])
        xs, st_s = _trunk_layer(xs, mod[nb_p:], cache_l, *args)
        states_s.append(st_s)
    sp = [t[0][None] for t in zip(*states_p)]
    ss = [t[0][None] for t in zip(*states_s)]
    return (xp, xs, *sp, *ss)
```

```python
import functools

import jax
import jax.numpy as jnp
import numpy as np
from jax import lax
from jax.experimental import pallas as pl
from jax.experimental.pallas import tpu as pltpu

F32 = jnp.float32
BF16 = jnp.bfloat16

CHUNK = 64
CHUNK_SHIFT = 6
HEAD_DIM = 128
N_HEADS_A = 8
N_IDX_HEADS = 16
IDX_DIM = 64
TOPK_MAX = 256
N_HEADS_B = 8
BAND_CHUNKS = 8
BAND_PAST = BAND_CHUNKS * CHUNK
REL_CLIP = 2 * CHUNK
CONV_W = 3
EPS = 1e-6
NEG = -1e30
W_A = N_HEADS_A * HEAD_DIM
W_B = N_HEADS_B * HEAD_DIM
ATTN_SCALE = HEAD_DIM ** -0.5
LOG2E = 1.4426950408889634
HALF_BITS = 16
HALF_MASK = 0xFFFF
HALF_BIAS = 1 << (HALF_BITS - 1)
LANES = 128
MXU_COLS = 256
VMEM_LIMIT = 56 * 1024 * 1024
FFN_ACC_BYTES = 8 * 1024 * 1024


def _cp(*sem):
    return pltpu.CompilerParams(dimension_semantics=sem, vmem_limit_bytes=VMEM_LIMIT)


def _pick(dim, pref, align):
    t = min(pref, dim)
    t -= t % align
    while t >= align:
        if dim % t == 0:
            return t
        t -= align
    return dim


def _ada_kernel(c_ref, w_ref, b_ref, o_ref):
    c = c_ref[...]
    s = c * jax.nn.sigmoid(c)
    o_ref[...] = jnp.dot(s, w_ref[...], preferred_element_type=F32,
                         precision=lax.Precision.HIGHEST) + b_ref[...]


def _ada(c, w, b):
    r, d = c.shape
    n = w.shape[1]
    tn = _pick(n, 1024, LANES)
    return pl.pallas_call(
        _ada_kernel,
        grid=(n // tn,),
        in_specs=[pl.BlockSpec((r, d), lambda j: (0, 0)),
                  pl.BlockSpec((d, tn), lambda j: (0, j)),
                  pl.BlockSpec((1, tn), lambda j: (0, j))],
        out_specs=pl.BlockSpec((r, tn), lambda j: (0, j)),
        out_shape=jax.ShapeDtypeStruct((r, n), F32),
        compiler_params=_cp("arbitrary"),
        name="ada",
    )(c, w, b.reshape(1, n))


def _rms_mod(x, g, shift, scale):
    y = x * lax.rsqrt(jnp.mean(x * x, axis=-1, keepdims=True) + EPS) * g
    return y * (1.0 + scale) + shift


def _split_bf16(x):
    hi = x.astype(BF16)
    return hi, (x - hi.astype(F32)).astype(BF16)


def _norm_mod_kernel(x_ref, g_ref, mod_ref, hi_ref, lo_ref):
    h = _rms_mod(x_ref[0], g_ref[...], mod_ref[0, 0:1, :], mod_ref[0, 1:2, :])
    hi, lo = _split_bf16(h)
    hi_ref[0] = hi
    lo_ref[0] = lo


def _norm_mod(x, g, mod):
    b, t, d = x.shape
    tm = _pick(t, 512, 16)
    row = lambda i, j: (i, j, 0)
    return pl.pallas_call(
        _norm_mod_kernel,
        grid=(b, t // tm),
        in_specs=[pl.BlockSpec((1, tm, d), row),
                  pl.BlockSpec((1, d), lambda i, j: (0, 0)),
                  pl.BlockSpec((1, 6, d), lambda i, j: (i, 0, 0))],
        out_specs=[pl.BlockSpec((1, tm, d), row), pl.BlockSpec((1, tm, d), row)],
        out_shape=[jax.ShapeDtypeStruct((b, t, d), BF16), jax.ShapeDtypeStruct((b, t, d), BF16)],
        compiler_params=_cp("arbitrary", "arbitrary"),
        name="norm_mod",
    )(x, g.reshape(1, d), mod)


def _mm_kernel(a_ref, w_ref, *o_refs):
    acc = jnp.dot(a_ref[...], w_ref[...], preferred_element_type=F32)
    for o_ref in o_refs:
        o_ref[...] = acc.astype(o_ref.dtype)


def _mm(a, w, dtypes):
    m, k = a.shape
    n = w.shape[1]
    tm = _pick(m, 1024, 16)
    tn = _pick(n, 1024, LANES)
    outs = pl.pallas_call(
        _mm_kernel,
        grid=(m // tm, n // tn),
        in_specs=[pl.BlockSpec((tm, k), lambda i, j: (i, 0)),
                  pl.BlockSpec((k, tn), lambda i, j: (0, j))],
        out_specs=[pl.BlockSpec((tm, tn), lambda i, j: (i, j)) for _ in dtypes],
        out_shape=[jax.ShapeDtypeStruct((m, n), dt) for dt in dtypes],
        compiler_params=_cp("arbitrary", "arbitrary"),
        name="proj",
    )(a, w)
    return outs


def _mm_split_kernel(ah_ref, al_ref, wh_ref, wl_ref, o_ref):
    ah = ah_ref[...]
    acc = jnp.dot(ah, wh_ref[...], preferred_element_type=F32)
    acc = acc + jnp.dot(al_ref[...], wh_ref[...], preferred_element_type=F32)
    acc = acc + jnp.dot(ah, wl_ref[...], preferred_element_type=F32)
    o_ref[...] = acc


def _mm_split(a_hi, a_lo, w_hi, w_lo):
    m, k = a_hi.shape
    n = w_hi.shape[1]
    tm = _pick(m, 1024, 16)
    tn = _pick(n, 1024, LANES)
    a_spec = pl.BlockSpec((tm, k), lambda i, j: (i, 0))
    w_spec = pl.BlockSpec((k, tn), lambda i, j: (0, j))
    return pl.pallas_call(
        _mm_split_kernel,
        grid=(m // tm, n // tn),
        in_specs=[a_spec, a_spec, w_spec, w_spec],
        out_specs=pl.BlockSpec((tm, tn), lambda i, j: (i, j)),
        out_shape=jax.ShapeDtypeStruct((m, n), F32),
        compiler_params=_cp("arbitrary", "arbitrary"),
        name="proj_split",
    )(a_hi, a_lo, w_hi, w_lo)


def _mixer_a_kernel(tile_ref, block_ref, qa_ref, qi_ref, wit_ref, ki2_ref, k_ref, vt_ref, o_ref,
                    key_sc, hi_sc, lo_sc, thr_sc, m_sc, l_sc, acc_sc, qx_sc,
                    *, tq, tk, past, valid_len, topk):
    i = tile_ref[pl.program_id(1)]
    j = block_ref[pl.program_id(1)]
    q0 = past + i * tq
    kend = jnp.minimum(((q0 + tq + CHUNK - 1) // CHUNK) * CHUNK, valid_len)
    nblk = (kend + tk - 1) // tk
    slab16 = 16

    def admissible(jb):
        kpos = jb * tk + lax.broadcasted_iota(jnp.int32, (tk, tq), 0)
        qpos = q0 + lax.broadcasted_iota(jnp.int32, (tk, tq), 1)
        adm = (jnp.right_shift(kpos, CHUNK_SHIFT) <= jnp.right_shift(qpos, CHUNK_SHIFT)) & (kpos < valid_len)
        return adm, qpos, kpos

    @pl.when(j == 0)
    def _index_and_threshold():
        qt = qi_ref[0].T
        q_hi, q_lo = _split_bf16(qt)
        for h in range(N_IDX_HEADS):
            rows = slice(h * IDX_DIM, (h + 1) * IDX_DIM)
            qx_sc[h, 0 * IDX_DIM:1 * IDX_DIM, :] = q_hi[rows]
            qx_sc[h, 1 * IDX_DIM:2 * IDX_DIM, :] = q_hi[rows]
            qx_sc[h, 2 * IDX_DIM:3 * IDX_DIM, :] = q_lo[rows]
            qx_sc[h, 3 * IDX_DIM:4 * IDX_DIM, :] = q_lo[rows]

        first = lax.broadcasted_iota(jnp.int32, (tk, LANES), 1) < IDX_DIM

        def score_block(jb, carry):
            dup = ki2_ref[0, jb]
            kk = jnp.where(first, dup, dup - dup.astype(BF16).astype(F32)).astype(BF16)
            kx = jnp.concatenate([kk, kk], axis=1)
            acc = jnp.zeros((tk, tq), F32)
            for h in range(N_IDX_HEADS):
                rel = jnp.dot(kx, qx_sc[h], preferred_element_type=F32)
                acc = acc + wit_ref[0, h:h + 1, :] * jnp.maximum(rel, 0.0)
            adm, _, _ = admissible(jb)
            acc = jnp.where(adm, acc, NEG)
            bits = pltpu.bitcast(acc, jnp.int32)
            key = bits ^ (jnp.right_shift(bits, 31) & 0x7FFFFFFF)
            key_sc[jb] = key
            hi_sc[jb] = jnp.right_shift(key, HALF_BITS).astype(jnp.int16)
            lo_sc[jb] = ((key & HALF_MASK) - HALF_BIAS).astype(jnp.int16)
            return carry

        lax.fori_loop(0, nblk, score_block, 0)

        def count_ge(src_sc, cand):
            cand16 = cand.astype(jnp.int16)

            def count_block(jb, cnt):
                part = [None] * 4
                for r in range(tk // slab16):
                    rows = slice(r * slab16, (r + 1) * slab16)
                    hit = jnp.where(src_sc[jb, rows, :] >= cand16, jnp.int16(1), jnp.int16(0))
                    part[r % 4] = hit if part[r % 4] is None else part[r % 4] + hit
                return cnt + ((part[0] + part[1]) + (part[2] + part[3]))

            cnt = lax.fori_loop(0, nblk, count_block, jnp.zeros((slab16, tq), jnp.int16))
            return jnp.sum(cnt.astype(jnp.int32), axis=0, keepdims=True)

        def kth_largest16(src_sc, want):
            def bit_step(b, t_u):
                cand_u = t_u | lax.shift_left(jnp.int32(1), HALF_BITS - 1 - b)
                return jnp.where(count_ge(src_sc, cand_u - HALF_BIAS) >= want, cand_u, t_u)
            t_u = lax.fori_loop(0, HALF_BITS, bit_step, jnp.zeros((slab16, tq), jnp.int32))
            return t_u - HALF_BIAS

        t_hi = kth_largest16(hi_sc, topk)
        above = jnp.where(t_hi == HALF_BIAS - 1, 0, count_ge(hi_sc, jnp.minimum(t_hi + 1, HALF_BIAS - 1)))
        t_hi16 = t_hi.astype(jnp.int16)

        def keep_group(jb, carry):
            for r in range(tk // slab16):
                rows = slice(r * slab16, (r + 1) * slab16)
                lo_sc[jb, rows, :] = jnp.where(hi_sc[jb, rows, :] == t_hi16, lo_sc[jb, rows, :],
                                               jnp.int16(-HALF_BIAS))
            return carry

        lax.fori_loop(0, nblk, keep_group, 0)
        t_lo = kth_largest16(lo_sc, topk - above)
        thr_sc[...] = (t_hi * (2 * HALF_BIAS) + (t_lo + HALF_BIAS))[0:8, :]
        m_sc[...] = jnp.full(m_sc.shape, -jnp.inf, F32)
        l_sc[...] = jnp.zeros(l_sc.shape, F32)
        acc_sc[...] = jnp.zeros(acc_sc.shape, F32)

    def _attend():
        adm, qpos, kpos = admissible(j)
        sel = (key_sc[j] >= thr_sc[0:1, :]) & adm
        dist = jnp.where(sel, jnp.abs(qpos - kpos).astype(F32), -NEG)

        nch = 2 if tk % (2 * MXU_COLS) == 0 else 1
        ck = tk // nch

        def logits(h, c):
            hs = slice(h * HEAD_DIM, (h + 1) * HEAD_DIM)
            return lax.dot_general(k_ref[0, c * ck:(c + 1) * ck, hs], qa_ref[0, :, hs],
                                   (((1,), (1,)), ((), ())), preferred_element_type=F32)

        s_next = [logits(0, c) for c in range(nch)]
        for h in range(N_HEADS_A):
            hs = slice(h * HEAD_DIM, (h + 1) * HEAD_DIM)
            s_raw, s_next = s_next, []
            m_prev = m_sc[h:h + 1, :]
            m_new = m_prev
            s = []
            for c in range(nch):
                if h + 1 < N_HEADS_A:
                    s_next.append(logits(h + 1, c))
                s.append(s_raw[c] * (ATTN_SCALE * LOG2E)
                         - ((2.0 ** -(h + 1)) * LOG2E) * dist[c * ck:(c + 1) * ck])
                m_new = jnp.maximum(m_new, jnp.max(s[c], axis=0, keepdims=True))
            alpha = jnp.exp2(m_prev - m_new)
            l_new = alpha * l_sc[h:h + 1, :]
            acc = alpha * acc_sc[hs, :]
            for c in range(nch):
                p = jnp.exp2(s[c] - m_new)
                l_new = l_new + jnp.sum(p, axis=0, keepdims=True)
                acc = acc + jnp.dot(vt_ref[0, hs, c * ck:(c + 1) * ck], p.astype(BF16),
                                    preferred_element_type=F32)
            l_sc[h:h + 1, :] = l_new
            acc_sc[hs, :] = acc
            m_sc[h:h + 1, :] = m_new

    _attend()

    @pl.when(j == nblk - 1)
    def _finalize():
        for h in range(N_HEADS_A):
            hs = slice(h * HEAD_DIM, (h + 1) * HEAD_DIM)
            o_ref[0, :, hs] = (acc_sc[hs, :] / l_sc[h:h + 1, :]).T.astype(o_ref.dtype)


def _mixer_a(q2, qi, wi, ki2, k_all, v_all, *, past, valid_len, topk, tq, tk):
    b, t, _ = q2.shape
    lp = k_all.shape[1]
    nkb = lp // tk
    assert lp % tk == 0 and t % tq == 0 and tk % LANES == 0 and past % CHUNK == 0
    ki2 = ki2.reshape(b, nkb, tk, ki2.shape[2])
    wit = jnp.swapaxes(wi, 1, 2)
    vt = jnp.swapaxes(v_all, 1, 2)

    def blocks_needed(i):
        kend = min(-(-(past + (i + 1) * tq) // CHUNK) * CHUNK, valid_len)
        return -(-kend // tk)

    pairs = [(i, j) for i in range(t // tq) for j in range(blocks_needed(i))]
    tile_of = jnp.asarray([p[0] for p in pairs], jnp.int32)
    block_of = jnp.asarray([p[1] for p in pairs], jnp.int32)
    q_map = lambda bb, p, ti, bj: (bb, ti[p], 0)
    kernel = functools.partial(_mixer_a_kernel, tq=tq, tk=tk, past=past, valid_len=valid_len, topk=topk)
    grid_spec = pltpu.PrefetchScalarGridSpec(
        num_scalar_prefetch=2,
        grid=(b, len(pairs)),
        in_specs=[pl.BlockSpec((1, tq, W_A), q_map),
                  pl.BlockSpec((1, tq, N_IDX_HEADS * IDX_DIM), q_map),
                  pl.BlockSpec((1, N_IDX_HEADS, tq), lambda bb, p, ti, bj: (bb, 0, ti[p])),
                  pl.BlockSpec((1, nkb, tk, 2 * IDX_DIM), lambda bb, p, ti, bj: (bb, 0, 0, 0)),
                  pl.BlockSpec((1, tk, W_A), lambda bb, p, ti, bj: (bb, bj[p], 0)),
                  pl.BlockSpec((1, W_A, tk), lambda bb, p, ti, bj: (bb, 0, bj[p]))],
        out_specs=pl.BlockSpec((1, tq, W_A), q_map),
        scratch_shapes=[pltpu.VMEM((nkb, tk, tq), jnp.int32),
                        pltpu.VMEM((nkb, tk, tq), jnp.int16),
                        pltpu.VMEM((nkb, tk, tq), jnp.int16),
                        pltpu.VMEM((8, tq), jnp.int32),
                        pltpu.VMEM((N_HEADS_A, tq), F32),
                        pltpu.VMEM((N_HEADS_A, tq), F32),
                        pltpu.VMEM((W_A, tq), F32),
                        pltpu.VMEM((N_IDX_HEADS, 4 * IDX_DIM, tq), BF16)])
    return pl.pallas_call(
        kernel,
        grid_spec=grid_spec,
        out_shape=jax.ShapeDtypeStruct((b, t, W_A), BF16),
        compiler_params=_cp("arbitrary", "arbitrary"),
        name="mixer_a",
    )(tile_of, block_of, q2, qi, wit, ki2, k_all, vt)


def _mixer_b_kernel(*refs, nkb, tkb):
    q_ref = refs[0]
    k_refs = refs[1:1 + nkb]
    v_refs = refs[1 + nkb:1 + 2 * nkb]
    bias_ref = refs[1 + 2 * nkb]
    o_ref = refs[2 + 2 * nkb]
    def logits(h):
        hs = slice(h * HEAD_DIM, (h + 1) * HEAD_DIM)
        return [lax.dot_general(kr[0, :, hs], q_ref[0, :, hs], (((1,), (1,)), ((), ())),
                                preferred_element_type=F32) for kr in k_refs]

    s_next = logits(0)
    for h in range(N_HEADS_B):
        hs = slice(h * HEAD_DIM, (h + 1) * HEAD_DIM)
        s_raw = s_next
        if h + 1 < N_HEADS_B:
            s_next = logits(h + 1)
        s = [s_raw[c] * (ATTN_SCALE * LOG2E) + bias_ref[0, h, c * tkb:(c + 1) * tkb, :] for c in range(nkb)]
        m = functools.reduce(jnp.maximum, [jnp.max(x, axis=0, keepdims=True) for x in s])
        l = None
        pv = None
        for c, vr in enumerate(v_refs):
            p = jnp.exp2(s[c] - m)
            lc = jnp.sum(p, axis=0, keepdims=True)
            t = lax.dot_general(vr[0, :, hs], p.astype(BF16), (((0,), (0,)), ((), ())),
                                preferred_element_type=F32)
            l = lc if l is None else l + lc
            pv = t if pv is None else pv + t
        o_ref[0, :, hs] = (pv / l).T.astype(o_ref.dtype)


def _band_bias(rel_bias, tq, wk, off, lows, hi):
    nh = rel_bias.shape[0]
    n = wk + tq
    e = np.concatenate([np.arange(tq + 1), np.arange(-(wk - 1), 0)])
    g = rel_bias[:, np.clip(off + e, -REL_CLIP, REL_CLIP) + REL_CLIP].astype(F32) * LOG2E
    bias = jnp.tile(g, (1, wk))[:, :wk * (n - 1)].reshape(nh, wk, n - 1)[:, :, :tq]
    c = np.arange(wk)[:, None]
    r = np.arange(tq)[None, :]
    dq = r // CHUNK
    dk = np.floor_divide(c - off, CHUNK)
    ok = (dk <= dq) & (dk >= dq - BAND_CHUNKS) & (c < hi)
    tiles = [jnp.where((ok & (c >= lo))[None], bias, NEG) for lo in lows]
    return jnp.stack(tiles)


def _mixer_b(q2, kv, bias, *, tq, tkb, nkb, back):
    b, t, _ = q2.shape
    nvar = bias.shape[0]
    wk = nkb * tkb
    kmaps = [functools.partial(lambda bb, i, p, col: (bb, jnp.maximum(i - back + p, 0), col), p=p, col=0)
             for p in range(nkb)]
    vmaps = [functools.partial(lambda bb, i, p, col: (bb, jnp.maximum(i - back + p, 0), col), p=p, col=1)
             for p in range(nkb)]
    kernel = functools.partial(_mixer_b_kernel, nkb=nkb, tkb=tkb)
    return pl.pallas_call(
        kernel,
        grid=(b, t // tq),
        in_specs=([pl.BlockSpec((1, tq, W_B), lambda bb, i: (bb, i, 1))]
                  + [pl.BlockSpec((1, tkb, W_B), m) for m in kmaps]
                  + [pl.BlockSpec((1, tkb, W_B), m) for m in vmaps]
                  + [pl.BlockSpec((1, N_HEADS_B, wk, tq), lambda bb, i: (jnp.minimum(i, nvar - 1), 0, 0, 0))]),
        out_specs=pl.BlockSpec((1, tq, W_B), lambda bb, i: (bb, i, 0)),
        out_shape=jax.ShapeDtypeStruct((b, t, W_B), BF16),
        compiler_params=_cp("arbitrary", "arbitrary"),
        name="mixer_b",
    )(q2, *([kv] * (2 * nkb)), bias)


def _merge_kernel(h_ref, oa_ref, ob_ref, pa_ref, pb_ref, wga_ref, wgb_ref, bga_ref, bgb_ref, o_ref):
    h = h_ref[...]
    ga = jax.nn.sigmoid(jnp.dot(h, wga_ref[...], preferred_element_type=F32) + bga_ref[...])
    gb = jax.nn.sigmoid(jnp.dot(h, wgb_ref[...], preferred_element_type=F32) + bgb_ref[...])
    ya = jnp.dot(oa_ref[...], pa_ref[...], preferred_element_type=F32)
    yb = jnp.dot(ob_ref[...], pb_ref[...], preferred_element_type=F32)
    o_ref[...] = (ga * ya + gb * yb).astype(o_ref.dtype)


def _merge(h, oa, ob, pa, pb, wg, bg):
    m, d = h.shape
    tm = _pick(m, 512, 16)
    tn = _pick(d, 512, LANES)
    nb = d // tn
    bg2 = bg.reshape(1, 2 * d)
    return pl.pallas_call(
        _merge_kernel,
        grid=(m // tm, nb),
        in_specs=[pl.BlockSpec((tm, d), lambda i, j: (i, 0)),
                  pl.BlockSpec((tm, W_A), lambda i, j: (i, 0)),
                  pl.BlockSpec((tm, W_B), lambda i, j: (i, 0)),
                  pl.BlockSpec((W_A, tn), lambda i, j: (0, j)),
                  pl.BlockSpec((W_B, tn), lambda i, j: (0, j)),
                  pl.BlockSpec((d, tn), lambda i, j: (0, j)),
                  pl.BlockSpec((d, tn), lambda i, j: (0, j + nb)),
                  pl.BlockSpec((1, tn), lambda i, j: (0, j)),
                  pl.BlockSpec((1, tn), lambda i, j: (0, j + nb))],
        out_specs=pl.BlockSpec((tm, tn), lambda i, j: (i, j)),
        out_shape=jax.ShapeDtypeStruct((m, d), BF16),
        compiler_params=_cp("arbitrary", "arbitrary"),
        name="merge",
    )(h, oa, ob, pa, pb, wg, wg, bg2, bg2)


def _outproj_kernel(mg_ref, x_ref, w_ref, mod_ref, g_ref, x1_ref, h2_ref):
    o = jnp.dot(mg_ref[0], w_ref[...], preferred_element_type=F32)
    x1 = x_ref[0] + mod_ref[0, 2:3, :] * o
    x1_ref[0] = x1
    h2_ref[0] = _rms_mod(x1, g_ref[...], mod_ref[0, 3:4, :], mod_ref[0, 4:5, :]).astype(h2_ref.dtype)


def _outproj(merged, x, w_out, mod, g2):
    b, t, d = x.shape
    tm = _pick(t, 256, 16)
    row = lambda i, j: (i, j, 0)
    return pl.pallas_call(
        _outproj_kernel,
        grid=(b, t // tm),
        in_specs=[pl.BlockSpec((1, tm, d), row),
                  pl.BlockSpec((1, tm, d), row),
                  pl.BlockSpec((d, d), lambda i, j: (0, 0)),
                  pl.BlockSpec((1, 6, d), lambda i, j: (i, 0, 0)),
                  pl.BlockSpec((1, d), lambda i, j: (0, 0))],
        out_specs=[pl.BlockSpec((1, tm, d), row), pl.BlockSpec((1, tm, d), row)],
        out_shape=[jax.ShapeDtypeStruct((b, t, d), F32), jax.ShapeDtypeStruct((b, t, d), BF16)],
        compiler_params=_cp("arbitrary", "arbitrary"),
        name="outproj",
    )(merged, x, w_out, mod, g2.reshape(1, d))


def _ffn_kernel(h2_ref, x1_ref, mod_ref, wa_ref, wb_ref, wd_ref, wc_ref, bc_ref, prev_ref, fg_ref,
                y_ref, st_ref, abuf, carry, acc_sc, *, tm, nf, weights_outer):
    if weights_outer:
        f, slot, m = pl.program_id(0), pl.program_id(1), pl.program_id(2)
    else:
        slot, m, f = 0, pl.program_id(1), pl.program_id(2)

    @pl.when(f == 0)
    def _():
        acc_sc[slot] = jnp.zeros(acc_sc.shape[1:], F32)

    h2 = h2_ref[0]
    prev = jnp.where(m == 0, prev_ref[0], carry[f])
    tf = wa_ref.shape[1]
    nsub = 2 if tf % (2 * MXU_COLS) == 0 else 1
    tfs = tf // nsub

    def up(c):
        cols = slice(c * tfs, (c + 1) * tfs)
        return (jnp.dot(h2, wa_ref[:, cols], preferred_element_type=F32),
                jnp.dot(h2, wb_ref[:, cols], preferred_element_type=F32))

    nxt = up(0)
    for c in range(nsub):
        cols = slice(c * tfs, (c + 1) * tfs)
        a, gate = nxt
        if c + 1 < nsub:
            nxt = up(c + 1)
        abuf[6:8, cols] = prev[:, cols]
        abuf[8:8 + tm, cols] = a
        tail = a[tm - 2:tm, :]
        carry[f, :, cols] = tail
        st_ref[0, 0, :, cols] = tail
        conv = (bc_ref[:, cols] + wc_ref[0:1, cols] * abuf[6:6 + tm, cols]
                + wc_ref[1:2, cols] * abuf[7:7 + tm, cols] + wc_ref[2:3, cols] * a)
        u = (conv * jax.nn.sigmoid(conv) * gate).astype(BF16)
        acc_sc[slot] += jnp.dot(u, wd_ref[cols, :], preferred_element_type=F32)

    @pl.when(f == nf - 1)
    def _():
        x2 = x1_ref[0] + mod_ref[0, 5:6, :] * acc_sc[slot]
        y_ref[0] = x2 * lax.rsqrt(jnp.mean(x2 * x2, axis=-1, keepdims=True) + EPS) * fg_ref[...]


def _ffn(h2, x1, mod, wa, wb, wd, wconv, bconv, prev, final_g):
    b, t, d = x1.shape
    dff = wa.shape[1]
    tm = _pick(t, 512, 16)
    tf = _pick(dff, 512, LANES)
    nf = dff // tf
    nm = t // tm
    weights_outer = nm == 1 and b * tm * d * 4 <= FFN_ACC_BYTES
    if weights_outer:
        grid = (nf, b, nm)
        spec = lambda shape, fn: pl.BlockSpec(shape, lambda k, i, j: fn(i, j, k))
        row = lambda i, j, k: (jnp.where(k == nf - 1, i, 0), jnp.where(k == nf - 1, j, 0), 0)
    else:
        grid = (b, nm, nf)
        spec = lambda shape, fn: pl.BlockSpec(shape, fn)
        row = lambda i, j, k: (i, j, 0)
    kernel = functools.partial(_ffn_kernel, tm=tm, nf=nf, weights_outer=weights_outer)
    return pl.pallas_call(
        kernel,
        grid=grid,
        in_specs=[spec((1, tm, d), lambda i, j, k: (i, j, 0)),
                  spec((1, tm, d), row),
                  spec((1, 6, d), lambda i, j, k: (i, 0, 0)),
                  spec((d, tf), lambda i, j, k: (0, k)),
                  spec((d, tf), lambda i, j, k: (0, k)),
                  spec((tf, d), lambda i, j, k: (k, 0)),
                  spec((CONV_W, tf), lambda i, j, k: (0, k)),
                  spec((1, tf), lambda i, j, k: (0, k)),
                  spec((1, CONV_W - 1, tf), lambda i, j, k: (i, 0, k)),
                  spec((1, d), lambda i, j, k: (0, 0))],
        out_specs=[spec((1, tm, d), row),
                   spec((1, 1, CONV_W - 1, tf), lambda i, j, k: (i, j, 0, k))],
        out_shape=[jax.ShapeDtypeStruct((b, t, d), F32),
                   jax.ShapeDtypeStruct((b, nm, CONV_W - 1, dff), F32)],
        scratch_shapes=[pltpu.VMEM((tm + 8, tf), F32),
                        pltpu.VMEM((nf, CONV_W - 1, tf), F32),
                        pltpu.VMEM((b if weights_outer else 1, tm, d), F32)],
        compiler_params=_cp("arbitrary", "arbitrary", "arbitrary"),
        name="ffn",
    )(h2, x1, mod, wa, wb, wd, wconv, bconv.reshape(1, dff), prev, final_g.reshape(1, d))


def _prep_weights(w_in, w_gate, w_proj_a, w_proj_b, w_out, w_up, w_down):
    o = np.cumsum((0, W_A, W_A, W_A, N_IDX_HEADS * IDX_DIM, IDX_DIM, N_IDX_HEADS, W_B, W_B, W_B))
    col = lambda a, b: w_in[:, int(o[a]):int(o[b])]
    d = w_in.shape[0]
    dff = w_down.shape[0]
    pad = jnp.zeros((d, MXU_COLS - 2 * IDX_DIM - N_IDX_HEADS), w_in.dtype)
    qi_hi, qi_lo = _split_bf16(col(3, 4))
    kiwi_hi, kiwi_lo = _split_bf16(jnp.concatenate([col(4, 5), col(4, 5), col(5, 6), pad], axis=1))
    return dict(
        q2=jnp.concatenate([col(0, 1), col(6, 7)], axis=1).astype(BF16),
        ka=col(1, 2).astype(BF16),
        va=col(2, 3).astype(BF16),
        qi_hi=qi_hi, qi_lo=qi_lo, kiwi_hi=kiwi_hi, kiwi_lo=kiwi_lo,
        kvb=col(7, 9).astype(BF16),
        gate=w_gate.astype(BF16),
        pa=w_proj_a.astype(BF16),
        pb=w_proj_b.astype(BF16),
        out=w_out.astype(BF16),
        up_a=w_up[:, :dff].astype(BF16),
        up_b=w_up[:, dff:].astype(BF16),
        down=w_down.astype(BF16),
    )


def _trunk_layer(x, mod, cache, wts, norm1_g, rel_bias, b_gate, norm2_g, w_conv, b_conv, final_g):
    b, t, d = x.shape
    dff = wts["down"].shape[0]
    h, h_lo = _norm_mod(x, norm1_g, mod)
    h2d = h.reshape(b * t, d)
    hl2d = h_lo.reshape(b * t, d)
    (q2,) = _mm(h2d, wts["q2"], (BF16,))
    ka32, ka16 = _mm(h2d, wts["ka"], (F32, BF16))
    va32, va16 = _mm(h2d, wts["va"], (F32, BF16))
    qi = _mm_split(h2d, hl2d, wts["qi_hi"], wts["qi_lo"]).reshape(b, t, N_IDX_HEADS * IDX_DIM)
    kiwi = _mm_split(h2d, hl2d, wts["kiwi_hi"], wts["kiwi_lo"])
    (kvb16,) = _mm(h2d, wts["kvb"], (BF16,))
    q2 = q2.reshape(b, t, W_A + W_B)
    kiwi = kiwi.reshape(b, t, MXU_COLS)
    ki = kiwi[:, :, :IDX_DIM]
    wi = kiwi[:, :, 2 * IDX_DIM:2 * IDX_DIM + N_IDX_HEADS]
    ka16 = ka16.reshape(b, t, W_A)
    va16 = va16.reshape(b, t, W_A)
    kvb16 = kvb16.reshape(b, t, 2 * W_B)

    if cache is None:
        rows_tail = min(BAND_PAST, t)
        tq_a = _pick(t, 256, CHUNK)
        tk_a = _pick(t, 512, LANES)
        o_a = _mixer_a(q2, qi, wi, kiwi, ka16, va16, past=0, valid_len=t, topk=min(TOPK_MAX, t // 4),
                       tq=tq_a, tk=tk_a)
        tq_b = _pick(t, 256, CHUNK)
        assert BAND_PAST % tq_b == 0
        back = BAND_PAST // tq_b
        lows = [BAND_PAST - v * tq_b for v in range(back + 1)]
        bias = _band_bias(rel_bias, tq_b, BAND_PAST + tq_b, BAND_PAST, lows, BAND_PAST + tq_b)
        o_b = _mixer_b(q2, kvb16, bias, tq=tq_b, tkb=tq_b, nkb=back + 1, back=back)
        conv_prev = jnp.zeros((b, CONV_W - 1, dff), F32)
    else:
        ck, cv, cki, cbk, cbv, conv_prev = cache
        past = ck.shape[1]
        rows_tail = t
        l_valid = past + t
        lp = -(-l_valid // LANES) * LANES
        tk_a = lp
        padk = jnp.zeros((b, lp - l_valid, W_A), BF16)
        k_all = jnp.concatenate([ck.reshape(b, past, W_A).astype(BF16), ka16, padk], axis=1)
        v_all = jnp.concatenate([cv.reshape(b, past, W_A).astype(BF16), va16, padk], axis=1)
        ki_all = jnp.concatenate([cki, ki, jnp.zeros((b, lp - l_valid, IDX_DIM), F32)], axis=1)
        ki_all = jnp.concatenate([ki_all, ki_all], axis=2)
        tq_a = -(-t // LANES) * LANES
        padq = lambda a: jnp.pad(a, ((0, 0), (0, tq_a - t), (0, 0)))
        o_a = _mixer_a(padq(q2), padq(qi), padq(wi), ki_all, k_all, v_all, past=past, valid_len=l_valid,
                       topk=min(TOPK_MAX, l_valid // 4), tq=tq_a, tk=tk_a)[:, :t]
        rows = cbk.shape[1]
        assert past % CHUNK == 0 and rows % CHUNK == 0
        lb_valid = rows + t
        lb = -(-lb_valid // LANES) * LANES
        kv_cache = jnp.concatenate([cbk.reshape(b, rows, W_B), cbv.reshape(b, rows, W_B)], axis=2).astype(BF16)
        kv_all = jnp.concatenate([kv_cache, kvb16, jnp.zeros((b, lb - lb_valid, 2 * W_B), BF16)], axis=1)
        bias = _band_bias(rel_bias, tq_a, lb, rows, [0], lb_valid)
        o_b = _mixer_b(padq(q2), kv_all, bias, tq=tq_a, tkb=lb, nkb=1, back=0)[:, :t]

    (tail32,) = _mm(h[:, t - rows_tail:].reshape(b * rows_tail, d), wts["kvb"], (F32,))
    tail32 = tail32.reshape(b, rows_tail, 2, N_HEADS_B, HEAD_DIM)
    new_bk, new_bv = tail32[:, :, 0], tail32[:, :, 1]

    merged = _merge(h2d, o_a.reshape(b * t, W_A), o_b.reshape(b * t, W_B), wts["pa"], wts["pb"],
                    wts["gate"], b_gate)
    x1, h2 = _outproj(merged.reshape(b, t, d), x, wts["out"], mod, norm2_g)
    y, tails = _ffn(h2, x1, mod, wts["up_a"], wts["up_b"], wts["down"], w_conv, b_conv,
                    conv_prev, final_g)
    conv_state = tails[:, -1]
    state = (ka32.reshape(b, t, N_HEADS_A, HEAD_DIM), va32.reshape(b, t, N_HEADS_A, HEAD_DIM), ki,
             new_bk, new_bv, conv_state)
    return y, state


def kernel(x_prompt, x_sample, cache_a_k, cache_a_v, cache_idx_k, cache_b_k, cache_b_v, state_ffn_conv,
           c_prompt, c_sample, w_ada, b_ada, norm1_g, w_in, rel_bias, w_gate, b_gate, w_proj_a, w_proj_b,
           w_out, norm2_g, w_up, w_conv, b_conv, w_down, final_g):
    depth = w_ada.shape[0]
    assert depth == 1, "the fused final RMSNorm assumes a single layer"
    d = x_prompt.shape[-1]
    nb_p = x_prompt.shape[0]
    c_all = jnp.concatenate([c_prompt, c_sample], axis=0)
    xp, xs = x_prompt, x_sample
    states_p, states_s = [], []
    for l in range(depth):
        wts = _prep_weights(w_in[l], w_gate[l], w_proj_a[l], w_proj_b[l], w_out[l], w_up[l], w_down[l])
        mod = _ada(c_all, w_ada[l], b_ada[l]).reshape(c_all.shape[0], 6, d)
        args = (wts, norm1_g[l], rel_bias[l], b_gate[l], norm2_g[l], w_conv[l], b_conv[l], final_g)
        xp, st_p = _trunk_layer(xp, mod[:nb_p], None, *args)
        states_p.append(st_p)
        cache_l = (cache_a_k[l], cache_a_v[l], cache_idx_k[l], cache_b_k[l], cache_b_v[l], state_ffn_conv[l])
        xs, st_s = _trunk_layer(xs, mod[nb_p:], cache_l, *args)
        states_s.append(st_s)
    sp = [t[0][None] for t in zip(*states_p)]
    ss = [t[0][None] for t in zip(*states_s)]
    return (xp, xs, *sp, *ss)
```

```python
import functools

import jax
import jax.numpy as jnp
import numpy as np
from jax import lax
from jax.experimental import pallas as pl
from jax.experimental.pallas import tpu as pltpu

F32 = jnp.float32
BF16 = jnp.bfloat16

CHUNK = 64
CHUNK_SHIFT = 6
HEAD_DIM = 128
N_HEADS_A = 8
N_IDX_HEADS = 16
IDX_DIM = 64
TOPK_MAX = 256
N_HEADS_B = 8
BAND_CHUNKS = 8
BAND_PAST = BAND_CHUNKS * CHUNK
REL_CLIP = 2 * CHUNK
CONV_W = 3
EPS = 1e-6
NEG = -1e30
W_A = N_HEADS_A * HEAD_DIM
W_B = N_HEADS_B * HEAD_DIM
ATTN_SCALE = HEAD_DIM ** -0.5
LOG2E = 1.4426950408889634
V_ROWS = HEAD_DIM + 16
HALF_BITS = 16
HALF_MASK = 0xFFFF
HALF_BIAS = 1 << (HALF_BITS - 1)
LANES = 128
MXU_COLS = 256
VMEM_LIMIT = 56 * 1024 * 1024
FFN_ACC_BYTES = 8 * 1024 * 1024


def _cp(*sem):
    return pltpu.CompilerParams(dimension_semantics=sem, vmem_limit_bytes=VMEM_LIMIT)


def _pick(dim, pref, align):
    t = min(pref, dim)
    t -= t % align
    while t >= align:
        if dim % t == 0:
            return t
        t -= align
    return dim


def _ada_kernel(c_ref, w_ref, b_ref, o_ref):
    c = c_ref[...]
    s = c * jax.nn.sigmoid(c)
    o_ref[...] = jnp.dot(s, w_ref[...], preferred_element_type=F32,
                         precision=lax.Precision.HIGHEST) + b_ref[...]


def _ada(c, w, b):
    r, d = c.shape
    n = w.shape[1]
    tn = _pick(n, 1024, LANES)
    return pl.pallas_call(
        _ada_kernel,
        grid=(n // tn,),
        in_specs=[pl.BlockSpec((r, d), lambda j: (0, 0)),
                  pl.BlockSpec((d, tn), lambda j: (0, j)),
                  pl.BlockSpec((1, tn), lambda j: (0, j))],
        out_specs=pl.BlockSpec((r, tn), lambda j: (0, j)),
        out_shape=jax.ShapeDtypeStruct((r, n), F32),
        compiler_params=_cp("arbitrary"),
        name="ada",
    )(c, w, b.reshape(1, n))


def _rms_mod(x, g, shift, scale):
    y = x * lax.rsqrt(jnp.mean(x * x, axis=-1, keepdims=True) + EPS) * g
    return y * (1.0 + scale) + shift


def _split_bf16(x):
    hi = x.astype(BF16)
    return hi, (x - hi.astype(F32)).astype(BF16)


def _norm_mod_kernel(x_ref, g_ref, mod_ref, hi_ref, lo_ref):
    h = _rms_mod(x_ref[0], g_ref[...], mod_ref[0, 0:1, :], mod_ref[0, 1:2, :])
    hi, lo = _split_bf16(h)
    hi_ref[0] = hi
    lo_ref[0] = lo


def _norm_mod(x, g, mod):
    b, t, d = x.shape
    tm = _pick(t, 512, 16)
    row = lambda i, j: (i, j, 0)
    return pl.pallas_call(
        _norm_mod_kernel,
        grid=(b, t // tm),
        in_specs=[pl.BlockSpec((1, tm, d), row),
                  pl.BlockSpec((1, d), lambda i, j: (0, 0)),
                  pl.BlockSpec((1, 6, d), lambda i, j: (i, 0, 0))],
        out_specs=[pl.BlockSpec((1, tm, d), row), pl.BlockSpec((1, tm, d), row)],
        out_shape=[jax.ShapeDtypeStruct((b, t, d), BF16), jax.ShapeDtypeStruct((b, t, d), BF16)],
        compiler_params=_cp("arbitrary", "arbitrary"),
        name="norm_mod",
    )(x, g.reshape(1, d), mod)


def _mm_kernel(a_ref, w_ref, *o_refs, scale):
    acc = jnp.dot(a_ref[...], w_ref[...], preferred_element_type=F32)
    if scale is not None:
        acc = acc * scale
    for o_ref in o_refs:
        o_ref[...] = acc.astype(o_ref.dtype)


def _mm(a, w, dtypes, scale=None):
    m, k = a.shape
    n = w.shape[1]
    tm = _pick(m, 1024, 16)
    tn = _pick(n, 1024, LANES)
    outs = pl.pallas_call(
        functools.partial(_mm_kernel, scale=scale),
        grid=(m // tm, n // tn),
        in_specs=[pl.BlockSpec((tm, k), lambda i, j: (i, 0)),
                  pl.BlockSpec((k, tn), lambda i, j: (0, j))],
        out_specs=[pl.BlockSpec((tm, tn), lambda i, j: (i, j)) for _ in dtypes],
        out_shape=[jax.ShapeDtypeStruct((m, n), dt) for dt in dtypes],
        compiler_params=_cp("arbitrary", "arbitrary"),
        name="proj",
    )(a, w)
    return outs


def _mm_split_kernel(ah_ref, al_ref, wh_ref, wl_ref, o_ref):
    ah = ah_ref[...]
    acc = jnp.dot(ah, wh_ref[...], preferred_element_type=F32)
    acc = acc + jnp.dot(al_ref[...], wh_ref[...], preferred_element_type=F32)
    acc = acc + jnp.dot(ah, wl_ref[...], preferred_element_type=F32)
    o_ref[...] = acc


def _mm_split(a_hi, a_lo, w_hi, w_lo):
    m, k = a_hi.shape
    n = w_hi.shape[1]
    tm = _pick(m, 1024, 16)
    tn = _pick(n, 1024, LANES)
    a_spec = pl.BlockSpec((tm, k), lambda i, j: (i, 0))
    w_spec = pl.BlockSpec((k, tn), lambda i, j: (0, j))
    return pl.pallas_call(
        _mm_split_kernel,
        grid=(m // tm, n // tn),
        in_specs=[a_spec, a_spec, w_spec, w_spec],
        out_specs=pl.BlockSpec((tm, tn), lambda i, j: (i, j)),
        out_shape=jax.ShapeDtypeStruct((m, n), F32),
        compiler_params=_cp("arbitrary", "arbitrary"),
        name="proj_split",
    )(a_hi, a_lo, w_hi, w_lo)


def _mixer_a_kernel(tile_ref, block_ref, qa_ref, qi_ref, wit_ref, ki2_ref, k_ref, vt_ref, o_ref,
                    key_sc, hi_sc, lo_sc, thr_sc, m_sc, acc_sc, qx_sc,
                    *, tq, tk, past, valid_len, topk):
    i = tile_ref[pl.program_id(1)]
    j = block_ref[pl.program_id(1)]
    q0 = past + i * tq
    kend = jnp.minimum(((q0 + tq + CHUNK - 1) // CHUNK) * CHUNK, valid_len)
    nblk = (kend + tk - 1) // tk
    slab16 = 16

    def admissible(jb):
        kpos = jb * tk + lax.broadcasted_iota(jnp.int32, (tk, tq), 0)
        qpos = q0 + lax.broadcasted_iota(jnp.int32, (tk, tq), 1)
        adm = (jnp.right_shift(kpos, CHUNK_SHIFT) <= jnp.right_shift(qpos, CHUNK_SHIFT)) & (kpos < valid_len)
        return adm, qpos, kpos

    @pl.when(j == 0)
    def _index_and_threshold():
        qt = qi_ref[0].T
        q_hi, q_lo = _split_bf16(qt)
        for h in range(N_IDX_HEADS):
            rows = slice(h * IDX_DIM, (h + 1) * IDX_DIM)
            qx_sc[h, 0 * IDX_DIM:1 * IDX_DIM, :] = q_hi[rows]
            qx_sc[h, 1 * IDX_DIM:2 * IDX_DIM, :] = q_hi[rows]
            qx_sc[h, 2 * IDX_DIM:3 * IDX_DIM, :] = q_lo[rows]
            qx_sc[h, 3 * IDX_DIM:4 * IDX_DIM, :] = q_lo[rows]

        first = lax.broadcasted_iota(jnp.int32, (tk, LANES), 1) < IDX_DIM

        def score_block(jb, carry):
            dup = ki2_ref[0, jb]
            kk = jnp.where(first, dup, dup - dup.astype(BF16).astype(F32)).astype(BF16)
            kx = jnp.concatenate([kk, kk], axis=1)
            acc = jnp.zeros((tk, tq), F32)
            for h in range(N_IDX_HEADS):
                rel = jnp.dot(kx, qx_sc[h], preferred_element_type=F32)
                acc = acc + wit_ref[0, h:h + 1, :] * jnp.maximum(rel, 0.0)
            adm, _, _ = admissible(jb)
            acc = jnp.where(adm, acc, NEG)
            bits = pltpu.bitcast(acc, jnp.int32)
            key = bits ^ (jnp.right_shift(bits, 31) & 0x7FFFFFFF)
            key_sc[jb] = key
            hi_sc[jb] = jnp.right_shift(key, HALF_BITS).astype(jnp.int16)
            lo_sc[jb] = ((key & HALF_MASK) - HALF_BIAS).astype(jnp.int16)
            return carry

        lax.fori_loop(0, nblk, score_block, 0)

        def count_ge(src_sc, cand):
            cand16 = cand.astype(jnp.int16)

            def count_block(jb, cnt):
                part = [None] * 4
                for r in range(tk // slab16):
                    rows = slice(r * slab16, (r + 1) * slab16)
                    hit = jnp.where(src_sc[jb, rows, :] >= cand16, jnp.int16(1), jnp.int16(0))
                    part[r % 4] = hit if part[r % 4] is None else part[r % 4] + hit
                return cnt + ((part[0] + part[1]) + (part[2] + part[3]))

            cnt = lax.fori_loop(0, nblk, count_block, jnp.zeros((slab16, tq), jnp.int16))
            return jnp.sum(cnt.astype(jnp.int32), axis=0, keepdims=True)

        def kth_largest16(src_sc, want):
            def bit_step(b, t_u):
                cand_u = t_u | lax.shift_left(jnp.int32(1), HALF_BITS - 1 - b)
                return jnp.where(count_ge(src_sc, cand_u - HALF_BIAS) >= want, cand_u, t_u)
            t_u = lax.fori_loop(0, HALF_BITS, bit_step, jnp.zeros((slab16, tq), jnp.int32))
            return t_u - HALF_BIAS

        t_hi = kth_largest16(hi_sc, topk)
        above = jnp.where(t_hi == HALF_BIAS - 1, 0, count_ge(hi_sc, jnp.minimum(t_hi + 1, HALF_BIAS - 1)))
        t_hi16 = t_hi.astype(jnp.int16)

        def keep_group(jb, carry):
            for r in range(tk // slab16):
                rows = slice(r * slab16, (r + 1) * slab16)
                lo_sc[jb, rows, :] = jnp.where(hi_sc[jb, rows, :] == t_hi16, lo_sc[jb, rows, :],
                                               jnp.int16(-HALF_BIAS))
            return carry

        lax.fori_loop(0, nblk, keep_group, 0)
        t_lo = kth_largest16(lo_sc, topk - above)
        thr_sc[...] = (t_hi * (2 * HALF_BIAS) + (t_lo + HALF_BIAS))[0:8, :]
        m_sc[...] = jnp.full(m_sc.shape, -jnp.inf, F32)
        acc_sc[...] = jnp.zeros(acc_sc.shape, F32)

    def _attend():
        adm, qpos, kpos = admissible(j)
        sel = (key_sc[j] >= thr_sc[0:1, :]) & adm
        dist = jnp.where(sel, jnp.abs(qpos - kpos).astype(F32), -NEG)

        nch = 2 if tk % (2 * MXU_COLS) == 0 else 1
        ck = tk // nch

        def logits(h, c):
            hs = slice(h * HEAD_DIM, (h + 1) * HEAD_DIM)
            return lax.dot_general(k_ref[0, c * ck:(c + 1) * ck, hs], qa_ref[0, :, hs],
                                   (((1,), (1,)), ((), ())), preferred_element_type=F32)

        s_next = [logits(0, c) for c in range(nch)]
        for h in range(N_HEADS_A):
            hx = slice(h * V_ROWS, (h + 1) * V_ROWS)
            s_raw, s_next = s_next, []
            m_prev = m_sc[h:h + 1, :]
            m_new = m_prev
            s = []
            for c in range(nch):
                if h + 1 < N_HEADS_A:
                    s_next.append(logits(h + 1, c))
                s.append(s_raw[c] - ((2.0 ** -(h + 1)) * LOG2E) * dist[c * ck:(c + 1) * ck])
                m_new = jnp.maximum(m_new, jnp.max(s[c], axis=0, keepdims=True))
            acc = jnp.exp2(m_prev - m_new) * acc_sc[hx, :]
            for c in range(nch):
                p = jnp.exp2(s[c] - m_new)
                acc = acc + jnp.dot(vt_ref[0, hx, c * ck:(c + 1) * ck], p.astype(BF16),
                                    preferred_element_type=F32)
            acc_sc[hx, :] = acc
            m_sc[h:h + 1, :] = m_new

    _attend()

    @pl.when(j == nblk - 1)
    def _finalize():
        for h in range(N_HEADS_A):
            hs = slice(h * HEAD_DIM, (h + 1) * HEAD_DIM)
            num = acc_sc[h * V_ROWS:h * V_ROWS + HEAD_DIM, :]
            den = acc_sc[h * V_ROWS + HEAD_DIM:h * V_ROWS + HEAD_DIM + 1, :]
            o_ref[0, :, hs] = (num / den).T.astype(o_ref.dtype)


def _mixer_a(q2, qi, wi, ki2, k_all, v_all, *, past, valid_len, topk, tq, tk):
    b, t, _ = q2.shape
    lp = k_all.shape[1]
    nkb = lp // tk
    assert lp % tk == 0 and t % tq == 0 and tk % LANES == 0 and past % CHUNK == 0
    ki2 = ki2.reshape(b, nkb, tk, ki2.shape[2])
    wit = jnp.swapaxes(wi, 1, 2)
    vt = jnp.swapaxes(v_all, 1, 2).reshape(b, N_HEADS_A, HEAD_DIM, lp)
    vt = jnp.concatenate([vt, jnp.ones((b, N_HEADS_A, V_ROWS - HEAD_DIM, lp), BF16)], axis=2)
    vt = vt.reshape(b, N_HEADS_A * V_ROWS, lp)

    def blocks_needed(i):
        kend = min(-(-(past + (i + 1) * tq) // CHUNK) * CHUNK, valid_len)
        return -(-kend // tk)

    pairs = [(i, j) for i in range(t // tq) for j in range(blocks_needed(i))]
    tile_of = jnp.asarray([p[0] for p in pairs], jnp.int32)
    block_of = jnp.asarray([p[1] for p in pairs], jnp.int32)
    q_map = lambda bb, p, ti, bj: (bb, ti[p], 0)
    kernel = functools.partial(_mixer_a_kernel, tq=tq, tk=tk, past=past, valid_len=valid_len, topk=topk)
    grid_spec = pltpu.PrefetchScalarGridSpec(
        num_scalar_prefetch=2,
        grid=(b, len(pairs)),
        in_specs=[pl.BlockSpec((1, tq, W_A), q_map),
                  pl.BlockSpec((1, tq, N_IDX_HEADS * IDX_DIM), q_map),
                  pl.BlockSpec((1, N_IDX_HEADS, tq), lambda bb, p, ti, bj: (bb, 0, ti[p])),
                  pl.BlockSpec((1, nkb, tk, 2 * IDX_DIM), lambda bb, p, ti, bj: (bb, 0, 0, 0)),
                  pl.BlockSpec((1, tk, W_A), lambda bb, p, ti, bj: (bb, bj[p], 0)),
                  pl.BlockSpec((1, N_HEADS_A * V_ROWS, tk), lambda bb, p, ti, bj: (bb, 0, bj[p]))],
        out_specs=pl.BlockSpec((1, tq, W_A), q_map),
        scratch_shapes=[pltpu.VMEM((nkb, tk, tq), jnp.int32),
                        pltpu.VMEM((nkb, tk, tq), jnp.int16),
                        pltpu.VMEM((nkb, tk, tq), jnp.int16),
                        pltpu.VMEM((8, tq), jnp.int32),
                        pltpu.VMEM((N_HEADS_A, tq), F32),
                        pltpu.VMEM((N_HEADS_A * V_ROWS, tq), F32),
                        pltpu.VMEM((N_IDX_HEADS, 4 * IDX_DIM, tq), BF16)])
    return pl.pallas_call(
        kernel,
        grid_spec=grid_spec,
        out_shape=jax.ShapeDtypeStruct((b, t, W_A), BF16),
        compiler_params=_cp("arbitrary", "arbitrary"),
        name="mixer_a",
    )(tile_of, block_of, q2, qi, wit, ki2, k_all, vt)


def _mixer_b_kernel(*refs, nkb, tkb):
    q_ref = refs[0]
    k_refs = refs[1:1 + nkb]
    v_refs = refs[1 + nkb:1 + 2 * nkb]
    bias_ref = refs[1 + 2 * nkb]
    o_ref = refs[2 + 2 * nkb]
    def logits(h, c):
        hs = slice(h * HEAD_DIM, (h + 1) * HEAD_DIM)
        return lax.dot_general(k_refs[c][0, :, hs], q_ref[0, :, hs], (((1,), (1,)), ((), ())),
                               preferred_element_type=F32)

    s_next = [logits(0, c) for c in range(nkb)]
    for h in range(N_HEADS_B):
        hs = slice(h * HEAD_DIM, (h + 1) * HEAD_DIM)
        s_raw, s_next = s_next, []
        s = []
        m = None
        for c in range(nkb):
            if h + 1 < N_HEADS_B:
                s_next.append(logits(h + 1, c))
            s.append(s_raw[c] * (ATTN_SCALE * LOG2E) + bias_ref[0, h, c * tkb:(c + 1) * tkb, :])
            mc = jnp.max(s[c], axis=0, keepdims=True)
            m = mc if m is None else jnp.maximum(m, mc)
        l = None
        pv = None
        for c, vr in enumerate(v_refs):
            p = jnp.exp2(s[c] - m)
            lc = jnp.sum(p, axis=0, keepdims=True)
            t = lax.dot_general(vr[0, :, hs], p.astype(BF16), (((0,), (0,)), ((), ())),
                                preferred_element_type=F32)
            l = lc if l is None else l + lc
            pv = t if pv is None else pv + t
        o_ref[0, :, hs] = (pv / l).T.astype(o_ref.dtype)


def _band_bias(rel_bias, tq, wk, off, lows, hi):
    nh = rel_bias.shape[0]
    n = wk + tq
    e = np.concatenate([np.arange(wk + 1), np.arange(-(tq - 1), 0)])
    f = rel_bias[:, np.clip(off - e, -REL_CLIP, REL_CLIP) + REL_CLIP].astype(F32) * LOG2E
    bias = jnp.tile(f, (1, tq))[:, :tq * (n - 1)].reshape(nh, tq, n - 1)[:, :, :wk]
    bias = jnp.swapaxes(bias, 1, 2)
    c = np.arange(wk)[:, None]
    r = np.arange(tq)[None, :]
    dq = r // CHUNK
    dk = np.floor_divide(c - off, CHUNK)
    ok = (dk <= dq) & (dk >= dq - BAND_CHUNKS) & (c < hi)
    tiles = [jnp.where((ok & (c >= lo))[None], bias, NEG) for lo in lows]
    return jnp.stack(tiles)


def _mixer_b(q2, kv, bias, *, tq, tkb, nkb, back):
    b, t, _ = q2.shape
    nvar = bias.shape[0]
    wk = nkb * tkb
    kmaps = [functools.partial(lambda bb, i, p, col: (bb, jnp.maximum(i - back + p, 0), col), p=p, col=0)
             for p in range(nkb)]
    vmaps = [functools.partial(lambda bb, i, p, col: (bb, jnp.maximum(i - back + p, 0), col), p=p, col=1)
             for p in range(nkb)]
    kernel = functools.partial(_mixer_b_kernel, nkb=nkb, tkb=tkb)
    return pl.pallas_call(
        kernel,
        grid=(b, t // tq),
        in_specs=([pl.BlockSpec((1, tq, W_B), lambda bb, i: (bb, i, 0))]
                  + [pl.BlockSpec((1, tkb, W_B), m) for m in kmaps]
                  + [pl.BlockSpec((1, tkb, W_B), m) for m in vmaps]
                  + [pl.BlockSpec((1, N_HEADS_B, wk, tq), lambda bb, i: (jnp.minimum(i, nvar - 1), 0, 0, 0))]),
        out_specs=pl.BlockSpec((1, tq, W_B), lambda bb, i: (bb, i, 0)),
        out_shape=jax.ShapeDtypeStruct((b, t, W_B), BF16),
        compiler_params=_cp("arbitrary", "arbitrary"),
        name="mixer_b",
    )(q2, *([kv] * (2 * nkb)), bias)


def _merge_kernel(h_ref, oa_ref, ob_ref, pa_ref, pb_ref, wga_ref, wgb_ref, bga_ref, bgb_ref, o_ref):
    h = h_ref[...]
    ga = jax.nn.sigmoid(jnp.dot(h, wga_ref[...], preferred_element_type=F32) + bga_ref[...])
    gb = jax.nn.sigmoid(jnp.dot(h, wgb_ref[...], preferred_element_type=F32) + bgb_ref[...])
    ya = jnp.dot(oa_ref[...], pa_ref[...], preferred_element_type=F32)
    yb = jnp.dot(ob_ref[...], pb_ref[...], preferred_element_type=F32)
    o_ref[...] = (ga * ya + gb * yb).astype(o_ref.dtype)


def _merge(h, oa, ob, pa, pb, wg, bg):
    m, d = h.shape
    tm = _pick(m, 512, 16)
    tn = _pick(d, 512, LANES)
    nb = d // tn
    bg2 = bg.reshape(1, 2 * d)
    return pl.pallas_call(
        _merge_kernel,
        grid=(m // tm, nb),
        in_specs=[pl.BlockSpec((tm, d), lambda i, j: (i, 0)),
                  pl.BlockSpec((tm, W_A), lambda i, j: (i, 0)),
                  pl.BlockSpec((tm, W_B), lambda i, j: (i, 0)),
                  pl.BlockSpec((W_A, tn), lambda i, j: (0, j)),
                  pl.BlockSpec((W_B, tn), lambda i, j: (0, j)),
                  pl.BlockSpec((d, tn), lambda i, j: (0, j)),
                  pl.BlockSpec((d, tn), lambda i, j: (0, j + nb)),
                  pl.BlockSpec((1, tn), lambda i, j: (0, j)),
                  pl.BlockSpec((1, tn), lambda i, j: (0, j + nb))],
        out_specs=pl.BlockSpec((tm, tn), lambda i, j: (i, j)),
        out_shape=jax.ShapeDtypeStruct((m, d), BF16),
        compiler_params=_cp("arbitrary", "arbitrary"),
        name="merge",
    )(h, oa, ob, pa, pb, wg, wg, bg2, bg2)


def _outproj_kernel(mg_ref, x_ref, w_ref, mod_ref, g_ref, x1_ref, h2_ref):
    o = jnp.dot(mg_ref[0], w_ref[...], preferred_element_type=F32)
    x1 = x_ref[0] + mod_ref[0, 2:3, :] * o
    x1_ref[0] = x1
    h2_ref[0] = _rms_mod(x1, g_ref[...], mod_ref[0, 3:4, :], mod_ref[0, 4:5, :]).astype(h2_ref.dtype)


def _outproj(merged, x, w_out, mod, g2):
    b, t, d = x.shape
    tm = _pick(t, 256, 16)
    row = lambda i, j: (i, j, 0)
    return pl.pallas_call(
        _outproj_kernel,
        grid=(b, t // tm),
        in_specs=[pl.BlockSpec((1, tm, d), row),
                  pl.BlockSpec((1, tm, d), row),
                  pl.BlockSpec((d, d), lambda i, j: (0, 0)),
                  pl.BlockSpec((1, 6, d), lambda i, j: (i, 0, 0)),
                  pl.BlockSpec((1, d), lambda i, j: (0, 0))],
        out_specs=[pl.BlockSpec((1, tm, d), row), pl.BlockSpec((1, tm, d), row)],
        out_shape=[jax.ShapeDtypeStruct((b, t, d), F32), jax.ShapeDtypeStruct((b, t, d), BF16)],
        compiler_params=_cp("arbitrary", "arbitrary"),
        name="outproj",
    )(merged, x, w_out, mod, g2.reshape(1, d))


def _ffn_kernel(h2_ref, x1_ref, mod_ref, wa_ref, wb_ref, wd_ref, wc_ref, bc_ref, prev_ref, fg_ref,
                y_ref, st_ref, abuf, carry, acc_sc, *, tm, nf, weights_outer):
    if weights_outer:
        f, slot, m = pl.program_id(0), pl.program_id(1), pl.program_id(2)
    else:
        slot, m, f = 0, pl.program_id(1), pl.program_id(2)

    @pl.when(f == 0)
    def _():
        acc_sc[slot] = jnp.zeros(acc_sc.shape[1:], F32)

    h2 = h2_ref[0]
    prev = jnp.where(m == 0, prev_ref[0], carry[f])
    tf = wa_ref.shape[1]
    nsub = 2 if tf % (2 * MXU_COLS) == 0 else 1
    tfs = tf // nsub

    def up(c):
        cols = slice(c * tfs, (c + 1) * tfs)
        return (jnp.dot(h2, wa_ref[:, cols], preferred_element_type=F32),
                jnp.dot(h2, wb_ref[:, cols], preferred_element_type=F32))

    nxt = up(0)
    for c in range(nsub):
        cols = slice(c * tfs, (c + 1) * tfs)
        a, gate = nxt
        if c + 1 < nsub:
            nxt = up(c + 1)
        abuf[6:8, cols] = prev[:, cols]
        abuf[8:8 + tm, cols] = a
        tail = a[tm - 2:tm, :]
        carry[f, :, cols] = tail
        st_ref[0, 0, :, cols] = tail
        conv = (bc_ref[:, cols] + wc_ref[0:1, cols] * abuf[6:6 + tm, cols]
                + wc_ref[1:2, cols] * abuf[7:7 + tm, cols] + wc_ref[2:3, cols] * a)
        u = (conv * jax.nn.sigmoid(conv) * gate).astype(BF16)
        acc_sc[slot] += jnp.dot(u, wd_ref[cols, :], preferred_element_type=F32)

    @pl.when(f == nf - 1)
    def _():
        x2 = x1_ref[0] + mod_ref[0, 5:6, :] * acc_sc[slot]
        y_ref[0] = x2 * lax.rsqrt(jnp.mean(x2 * x2, axis=-1, keepdims=True) + EPS) * fg_ref[...]


def _ffn(h2, x1, mod, wa, wb, wd, wconv, bconv, prev, final_g):
    b, t, d = x1.shape
    dff = wa.shape[1]
    tm = _pick(t, 512, 16)
    tf = _pick(dff, 512, LANES)
    nf = dff // tf
    nm = t // tm
    weights_outer = nm == 1 and b * tm * d * 4 <= FFN_ACC_BYTES
    if weights_outer:
        grid = (nf, b, nm)
        spec = lambda shape, fn: pl.BlockSpec(shape, lambda k, i, j: fn(i, j, k))
        row = lambda i, j, k: (jnp.where(k == nf - 1, i, 0), jnp.where(k == nf - 1, j, 0), 0)
    else:
        grid = (b, nm, nf)
        spec = lambda shape, fn: pl.BlockSpec(shape, fn)
        row = lambda i, j, k: (i, j, 0)
    kernel = functools.partial(_ffn_kernel, tm=tm, nf=nf, weights_outer=weights_outer)
    return pl.pallas_call(
        kernel,
        grid=grid,
        in_specs=[spec((1, tm, d), lambda i, j, k: (i, j, 0)),
                  spec((1, tm, d), row),
                  spec((1, 6, d), lambda i, j, k: (i, 0, 0)),
                  spec((d, tf), lambda i, j, k: (0, k)),
                  spec((d, tf), lambda i, j, k: (0, k)),
                  spec((tf, d), lambda i, j, k: (k, 0)),
                  spec((CONV_W, tf), lambda i, j, k: (0, k)),
                  spec((1, tf), lambda i, j, k: (0, k)),
                  spec((1, CONV_W - 1, tf), lambda i, j, k: (i, 0, k)),
                  spec((1, d), lambda i, j, k: (0, 0))],
        out_specs=[spec((1, tm, d), row),
                   spec((1, 1, CONV_W - 1, tf), lambda i, j, k: (i, j, 0, k))],
        out_shape=[jax.ShapeDtypeStruct((b, t, d), F32),
                   jax.ShapeDtypeStruct((b, nm, CONV_W - 1, dff), F32)],
        scratch_shapes=[pltpu.VMEM((tm + 8, tf), F32),
                        pltpu.VMEM((nf, CONV_W - 1, tf), F32),
                        pltpu.VMEM((b if weights_outer else 1, tm, d), F32)],
        compiler_params=_cp("arbitrary", "arbitrary", "arbitrary"),
        name="ffn",
    )(h2, x1, mod, wa, wb, wd, wconv, bconv.reshape(1, dff), prev, final_g.reshape(1, d))


def _prep_weights(w_in, w_gate, w_proj_a, w_proj_b, w_out, w_up, w_down):
    o = np.cumsum((0, W_A, W_A, W_A, N_IDX_HEADS * IDX_DIM, IDX_DIM, N_IDX_HEADS, W_B, W_B, W_B))
    col = lambda a, b: w_in[:, int(o[a]):int(o[b])]
    d = w_in.shape[0]
    dff = w_down.shape[0]
    pad = jnp.zeros((d, MXU_COLS - 2 * IDX_DIM - N_IDX_HEADS), w_in.dtype)
    qi_hi, qi_lo = _split_bf16(col(3, 4))
    kiwi_hi, kiwi_lo = _split_bf16(jnp.concatenate([col(4, 5), col(4, 5), col(5, 6), pad], axis=1))
    return dict(
        qa=col(0, 1).astype(BF16),
        qb=col(6, 7).astype(BF16),
        ka=col(1, 2).astype(BF16),
        va=col(2, 3).astype(BF16),
        qi_hi=qi_hi, qi_lo=qi_lo, kiwi_hi=kiwi_hi, kiwi_lo=kiwi_lo,
        kvb=col(7, 9).astype(BF16),
        gate=w_gate.astype(BF16),
        pa=w_proj_a.astype(BF16),
        pb=w_proj_b.astype(BF16),
        out=w_out.astype(BF16),
        up_a=w_up[:, :dff].astype(BF16),
        up_b=w_up[:, dff:].astype(BF16),
        down=w_down.astype(BF16),
    )


def _trunk_layer(x, mod, cache, wts, norm1_g, rel_bias, b_gate, norm2_g, w_conv, b_conv, final_g):
    b, t, d = x.shape
    dff = wts["down"].shape[0]
    h, h_lo = _norm_mod(x, norm1_g, mod)
    h2d = h.reshape(b * t, d)
    hl2d = h_lo.reshape(b * t, d)
    (qa,) = _mm(h2d, wts["qa"], (BF16,), scale=ATTN_SCALE * LOG2E)
    (qb,) = _mm(h2d, wts["qb"], (BF16,))
    ka32, ka16 = _mm(h2d, wts["ka"], (F32, BF16))
    va32, va16 = _mm(h2d, wts["va"], (F32, BF16))
    qi = _mm_split(h2d, hl2d, wts["qi_hi"], wts["qi_lo"]).reshape(b, t, N_IDX_HEADS * IDX_DIM)
    kiwi = _mm_split(h2d, hl2d, wts["kiwi_hi"], wts["kiwi_lo"])
    (kvb16,) = _mm(h2d, wts["kvb"], (BF16,))
    qa = qa.reshape(b, t, W_A)
    qb = qb.reshape(b, t, W_B)
    kiwi = kiwi.reshape(b, t, MXU_COLS)
    ki = kiwi[:, :, :IDX_DIM]
    wi = kiwi[:, :, 2 * IDX_DIM:2 * IDX_DIM + N_IDX_HEADS]
    ka16 = ka16.reshape(b, t, W_A)
    va16 = va16.reshape(b, t, W_A)
    kvb16 = kvb16.reshape(b, t, 2 * W_B)

    if cache is None:
        rows_tail = min(BAND_PAST, t)
        tq_a = _pick(t, 256, CHUNK)
        tk_a = _pick(t, 512, LANES)
        o_a = _mixer_a(qa, qi, wi, kiwi, ka16, va16, past=0, valid_len=t, topk=min(TOPK_MAX, t // 4),
                       tq=tq_a, tk=tk_a)
        tq_b = _pick(t, 256, CHUNK)
        assert BAND_PAST % tq_b == 0
        back = BAND_PAST // tq_b
        lows = [BAND_PAST - v * tq_b for v in range(back + 1)]
        bias = _band_bias(rel_bias, tq_b, BAND_PAST + tq_b, BAND_PAST, lows, BAND_PAST + tq_b)
        o_b = _mixer_b(qb, kvb16, bias, tq=tq_b, tkb=tq_b, nkb=back + 1, back=back)
        conv_prev = jnp.zeros((b, CONV_W - 1, dff), F32)
    else:
        ck, cv, cki, cbk, cbv, conv_prev = cache
        past = ck.shape[1]
        rows_tail = t
        l_valid = past + t
        lp = -(-l_valid // LANES) * LANES
        tk_a = lp
        padk = jnp.zeros((b, lp - l_valid, W_A), BF16)
        k_all = jnp.concatenate([ck.reshape(b, past, W_A).astype(BF16), ka16, padk], axis=1)
        v_all = jnp.concatenate([cv.reshape(b, past, W_A).astype(BF16), va16, padk], axis=1)
        ki_all = jnp.concatenate([cki, ki, jnp.zeros((b, lp - l_valid, IDX_DIM), F32)], axis=1)
        ki_all = jnp.concatenate([ki_all, ki_all], axis=2)
        tq_a = -(-t // LANES) * LANES
        padq = lambda a: jnp.pad(a, ((0, 0), (0, tq_a - t), (0, 0)))
        o_a = _mixer_a(padq(qa), padq(qi), padq(wi), ki_all, k_all, v_all, past=past, valid_len=l_valid,
                       topk=min(TOPK_MAX, l_valid // 4), tq=tq_a, tk=tk_a)[:, :t]
        rows = cbk.shape[1]
        assert past % CHUNK == 0 and rows % CHUNK == 0
        lb_valid = rows + t
        lb = -(-lb_valid // LANES) * LANES
        kv_cache = jnp.concatenate([cbk.reshape(b, rows, W_B), cbv.reshape(b, rows, W_B)], axis=2).astype(BF16)
        kv_all = jnp.concatenate([kv_cache, kvb16, jnp.zeros((b, lb - lb_valid, 2 * W_B), BF16)], axis=1)
        bias = _band_bias(rel_bias, tq_a, lb, rows, [0], lb_valid)
        o_b = _mixer_b(padq(qb), kv_all, bias, tq=tq_a, tkb=lb, nkb=1, back=0)[:, :t]

    (tail32,) = _mm(h[:, t - rows_tail:].reshape(b * rows_tail, d), wts["kvb"], (F32,))
    tail32 = tail32.reshape(b, rows_tail, 2, N_HEADS_B, HEAD_DIM)
    new_bk, new_bv = tail32[:, :, 0], tail32[:, :, 1]

    merged = _merge(h2d, o_a.reshape(b * t, W_A), o_b.reshape(b * t, W_B), wts["pa"], wts["pb"],
                    wts["gate"], b_gate)
    x1, h2 = _outproj(merged.reshape(b, t, d), x, wts["out"], mod, norm2_g)
    y, tails = _ffn(h2, x1, mod, wts["up_a"], wts["up_b"], wts["down"], w_conv, b_conv,
                    conv_prev, final_g)
    conv_state = tails[:, -1]
    state = (ka32.reshape(b, t, N_HEADS_A, HEAD_DIM), va32.reshape(b, t, N_HEADS_A, HEAD_DIM), ki,
             new_bk, new_bv, conv_state)
    return y, state


def kernel(x_prompt, x_sample, cache_a_k, cache_a_v, cache_idx_k, cache_b_k, cache_b_v, state_ffn_conv,
           c_prompt, c_sample, w_ada, b_ada, norm1_g, w_in, rel_bias, w_gate, b_gate, w_proj_a, w_proj_b,
           w_out, norm2_g, w_up, w_conv, b_conv, w_down, final_g):
    depth = w_ada.shape[0]
    assert depth == 1, "the fused final RMSNorm assumes a single layer"
    d = x_prompt.shape[-1]
    nb_p = x_prompt.shape[0]
    c_all = jnp.concatenate([c_prompt, c_sample], axis=0)
    xp, xs = x_prompt, x_sample
    states_p, states_s = [], []
    for l in range(depth):
        wts = _prep_weights(w_in[l], w_gate[l], w_proj_a[l], w_proj_b[l], w_out[l], w_up[l], w_down[l])
        mod = _ada(c_all, w_ada[l], b_ada[l]).reshape(c_all.shape[0], 6, d)
        args = (wts, norm1_g[l], rel_bias[l], b_gate[l], norm2_g[l], w_conv[l], b_conv[l], final_g)
        xp, st_p = _trunk_layer(xp, mod[:nb_p], None, *args)
        states_p.append(st_p)
        cache_l = (cache_a_k[l], cache_a_v[l], cache_idx_k[l], cache_b_k[l], cache_b_v[l], state_ffn_conv[l])
        xs, st_s = _trunk_layer(xs, mod[nb_p:], cache_l, *args)
        states_s.append(st_s)
    sp = [t[0][None] for t in zip(*states_p)]
    ss = [t[0][None] for t in zip(*states_s)]
    return (xp, xs, *sp, *ss)
```

```python
import functools

import jax
import jax.numpy as jnp
import numpy as np
from jax import lax
from jax.experimental import pallas as pl
from jax.experimental.pallas import tpu as pltpu

F32 = jnp.float32
BF16 = jnp.bfloat16

CHUNK = 64
CHUNK_SHIFT = 6
HEAD_DIM = 128
N_HEADS_A = 8
N_IDX_HEADS = 16
IDX_DIM = 64
TOPK_MAX = 256
N_HEADS_B = 8
BAND_CHUNKS = 8
BAND_PAST = BAND_CHUNKS * CHUNK
REL_CLIP = 2 * CHUNK
CONV_W = 3
EPS = 1e-6
NEG = -1e30
W_A = N_HEADS_A * HEAD_DIM
W_B = N_HEADS_B * HEAD_DIM
ATTN_SCALE = HEAD_DIM ** -0.5
LOG2E = 1.4426950408889634
V_ROWS = HEAD_DIM + 16
HALF_BITS = 16
HALF_MASK = 0xFFFF
HALF_BIAS = 1 << (HALF_BITS - 1)
LANES = 128
MXU_COLS = 256
VMEM_LIMIT = 56 * 1024 * 1024
FFN_ACC_BYTES = 8 * 1024 * 1024


def _cp(*sem):
    return pltpu.CompilerParams(dimension_semantics=sem, vmem_limit_bytes=VMEM_LIMIT)


def _pick(dim, pref, align):
    t = min(pref, dim)
    t -= t % align
    while t >= align:
        if dim % t == 0:
            return t
        t -= align
    return dim


def _ada_kernel(c_ref, w_ref, b_ref, o_ref):
    c = c_ref[...]
    s = c * jax.nn.sigmoid(c)
    o_ref[...] = jnp.dot(s, w_ref[...], preferred_element_type=F32,
                         precision=lax.Precision.HIGHEST) + b_ref[...]


def _ada(c, w, b):
    r, d = c.shape
    n = w.shape[1]
    tn = _pick(n, 1024, LANES)
    return pl.pallas_call(
        _ada_kernel,
        grid=(n // tn,),
        in_specs=[pl.BlockSpec((r, d), lambda j: (0, 0)),
                  pl.BlockSpec((d, tn), lambda j: (0, j)),
                  pl.BlockSpec((1, tn), lambda j: (0, j))],
        out_specs=pl.BlockSpec((r, tn), lambda j: (0, j)),
        out_shape=jax.ShapeDtypeStruct((r, n), F32),
        compiler_params=_cp("arbitrary"),
        name="ada",
    )(c, w, b.reshape(1, n))


def _rms_mod(x, g, shift, scale):
    y = x * lax.rsqrt(jnp.mean(x * x, axis=-1, keepdims=True) + EPS) * g
    return y * (1.0 + scale) + shift


def _split_bf16(x):
    hi = x.astype(BF16)
    return hi, (x - hi.astype(F32)).astype(BF16)


def _norm_mod_kernel(x_ref, g_ref, mod_ref, hi_ref, lo_ref):
    h = _rms_mod(x_ref[0], g_ref[...], mod_ref[0, 0:1, :], mod_ref[0, 1:2, :])
    hi, lo = _split_bf16(h)
    hi_ref[0] = hi
    lo_ref[0] = lo


def _norm_mod(x, g, mod):
    b, t, d = x.shape
    tm = _pick(t, 512, 16)
    row = lambda i, j: (i, j, 0)
    return pl.pallas_call(
        _norm_mod_kernel,
        grid=(b, t // tm),
        in_specs=[pl.BlockSpec((1, tm, d), row),
                  pl.BlockSpec((1, d), lambda i, j: (0, 0)),
                  pl.BlockSpec((1, 6, d), lambda i, j: (i, 0, 0))],
        out_specs=[pl.BlockSpec((1, tm, d), row), pl.BlockSpec((1, tm, d), row)],
        out_shape=[jax.ShapeDtypeStruct((b, t, d), BF16), jax.ShapeDtypeStruct((b, t, d), BF16)],
        compiler_params=_cp("arbitrary", "arbitrary"),
        name="norm_mod",
    )(x, g.reshape(1, d), mod)


def _mm_kernel(a_ref, w_ref, *o_refs, scale):
    acc = jnp.dot(a_ref[...], w_ref[...], preferred_element_type=F32)
    if scale is not None:
        acc = acc * scale
    for o_ref in o_refs:
        o_ref[...] = acc.astype(o_ref.dtype).reshape(o_ref.shape)


def _mm(a, w, dtypes, scale=None, heads_out=()):
    m, k = a.shape
    n = w.shape[1]
    tm = _pick(m, 1024, 16)
    tn = _pick(n, 1024, LANES)
    nh = tn // HEAD_DIM
    specs = [pl.BlockSpec((tm, nh, HEAD_DIM), lambda i, j: (i, j, 0)) if o in heads_out
             else pl.BlockSpec((tm, tn), lambda i, j: (i, j)) for o in range(len(dtypes))]
    shapes = [jax.ShapeDtypeStruct((m, n // HEAD_DIM, HEAD_DIM) if o in heads_out else (m, n), dt)
              for o, dt in enumerate(dtypes)]
    outs = pl.pallas_call(
        functools.partial(_mm_kernel, scale=scale),
        grid=(m // tm, n // tn),
        in_specs=[pl.BlockSpec((tm, k), lambda i, j: (i, 0)),
                  pl.BlockSpec((k, tn), lambda i, j: (0, j))],
        out_specs=specs,
        out_shape=shapes,
        compiler_params=_cp("arbitrary", "arbitrary"),
        name="proj",
    )(a, w)
    return outs


def _mm_split_kernel(ah_ref, al_ref, wh_ref, wl_ref, o_ref):
    ah = ah_ref[...]
    acc = jnp.dot(ah, wh_ref[...], preferred_element_type=F32)
    acc = acc + jnp.dot(al_ref[...], wh_ref[...], preferred_element_type=F32)
    acc = acc + jnp.dot(ah, wl_ref[...], preferred_element_type=F32)
    o_ref[...] = acc


def _mm_split(a_hi, a_lo, w_hi, w_lo):
    m, k = a_hi.shape
    n = w_hi.shape[1]
    tm = _pick(m, 1024, 16)
    tn = _pick(n, 1024, LANES)
    a_spec = pl.BlockSpec((tm, k), lambda i, j: (i, 0))
    w_spec = pl.BlockSpec((k, tn), lambda i, j: (0, j))
    return pl.pallas_call(
        _mm_split_kernel,
        grid=(m // tm, n // tn),
        in_specs=[a_spec, a_spec, w_spec, w_spec],
        out_specs=pl.BlockSpec((tm, tn), lambda i, j: (i, j)),
        out_shape=jax.ShapeDtypeStruct((m, n), F32),
        compiler_params=_cp("arbitrary", "arbitrary"),
        name="proj_split",
    )(a_hi, a_lo, w_hi, w_lo)


def _mixer_a_kernel(tile_ref, block_ref, qa_ref, qi_ref, wit_ref, ki2_ref, k_ref, vt_ref, o_ref,
                    key_sc, hi_sc, lo_sc, thr_sc, m_sc, acc_sc, qx_sc,
                    *, tq, tk, past, valid_len, topk):
    i = tile_ref[pl.program_id(1)]
    j = block_ref[pl.program_id(1)]
    q0 = past + i * tq
    kend = jnp.minimum(((q0 + tq + CHUNK - 1) // CHUNK) * CHUNK, valid_len)
    nblk = (kend + tk - 1) // tk
    slab16 = 16

    def admissible(jb, row0=0, rows=tk):
        kpos = jb * tk + row0 + lax.broadcasted_iota(jnp.int32, (rows, tq), 0)
        qpos = q0 + lax.broadcasted_iota(jnp.int32, (rows, tq), 1)
        adm = (jnp.right_shift(kpos, CHUNK_SHIFT) <= jnp.right_shift(qpos, CHUNK_SHIFT)) & (kpos < valid_len)
        return adm, qpos, kpos

    @pl.when(j == 0)
    def _index_and_threshold():
        qt = qi_ref[0].T
        q_hi, q_lo = _split_bf16(qt)
        for h in range(N_IDX_HEADS):
            rows = slice(h * IDX_DIM, (h + 1) * IDX_DIM)
            qx_sc[h, 0 * IDX_DIM:1 * IDX_DIM, :] = q_hi[rows]
            qx_sc[h, 1 * IDX_DIM:2 * IDX_DIM, :] = q_hi[rows]
            qx_sc[h, 2 * IDX_DIM:3 * IDX_DIM, :] = q_lo[rows]
            qx_sc[h, 3 * IDX_DIM:4 * IDX_DIM, :] = q_lo[rows]


        nsplit = 2 if tk % (2 * MXU_COLS) == 0 else 1
        ts = tk // nsplit
        first = lax.broadcasted_iota(jnp.int32, (ts, LANES), 1) < IDX_DIM

        def score_block(jb, carry):
            for r in range(nsplit):
                rows = slice(r * ts, (r + 1) * ts)
                dup = ki2_ref[0, jb, rows, :]
                kk = jnp.where(first, dup, dup - dup.astype(BF16).astype(F32)).astype(BF16)
                kx = jnp.concatenate([kk, kk], axis=1)
                acc = jnp.zeros((ts, tq), F32)
                for h in range(N_IDX_HEADS):
                    rel = jnp.dot(kx, qx_sc[h], preferred_element_type=F32)
                    acc = acc + wit_ref[0, h:h + 1, :] * jnp.maximum(rel, 0.0)
                acc = jnp.where(admissible(jb, r * ts, ts)[0], acc, NEG)
                bits = pltpu.bitcast(acc, jnp.int32)
                key = bits ^ (jnp.right_shift(bits, 31) & 0x7FFFFFFF)
                key_sc[jb, rows, :] = key
                hi_sc[jb, rows, :] = jnp.right_shift(key, HALF_BITS).astype(jnp.int16)
                lo_sc[jb, rows, :] = ((key & HALF_MASK) - HALF_BIAS).astype(jnp.int16)
            return carry

        lax.fori_loop(0, nblk, score_block, 0)

        def count_ge(src_sc, cand):
            cand16 = cand.astype(jnp.int16)

            def count_block(jb, cnt):
                part = [None] * 4
                for r in range(tk // slab16):
                    rows = slice(r * slab16, (r + 1) * slab16)
                    hit = jnp.where(src_sc[jb, rows, :] >= cand16, jnp.int16(1), jnp.int16(0))
                    part[r % 4] = hit if part[r % 4] is None else part[r % 4] + hit
                return cnt + ((part[0] + part[1]) + (part[2] + part[3]))

            cnt = lax.fori_loop(0, nblk, count_block, jnp.zeros((slab16, tq), jnp.int16))
            return jnp.sum(cnt.astype(jnp.int32), axis=0, keepdims=True)

        def kth_largest16(src_sc, want):
            def bit_step(b, t_u):
                cand_u = t_u | lax.shift_left(jnp.int32(1), HALF_BITS - 1 - b)
                return jnp.where(count_ge(src_sc, cand_u - HALF_BIAS) >= want, cand_u, t_u)
            t_u = lax.fori_loop(0, HALF_BITS, bit_step, jnp.zeros((slab16, tq), jnp.int32))
            return t_u - HALF_BIAS

        t_hi = kth_largest16(hi_sc, topk)
        above = jnp.where(t_hi == HALF_BIAS - 1, 0, count_ge(hi_sc, jnp.minimum(t_hi + 1, HALF_BIAS - 1)))
        t_hi16 = t_hi.astype(jnp.int16)

        def keep_group(jb, carry):
            for r in range(tk // slab16):
                rows = slice(r * slab16, (r + 1) * slab16)
                lo_sc[jb, rows, :] = jnp.where(hi_sc[jb, rows, :] == t_hi16, lo_sc[jb, rows, :],
                                               jnp.int16(-HALF_BIAS))
            return carry

        lax.fori_loop(0, nblk, keep_group, 0)
        t_lo = kth_largest16(lo_sc, topk - above)
        thr_sc[...] = (t_hi * (2 * HALF_BIAS) + (t_lo + HALF_BIAS))[0:8, :]
        m_sc[...] = jnp.full(m_sc.shape, -jnp.inf, F32)
        acc_sc[...] = jnp.zeros(acc_sc.shape, F32)

    def _attend():
        adm, qpos, kpos = admissible(j)
        sel = (key_sc[j] >= thr_sc[0:1, :]) & adm
        dist = jnp.where(sel, jnp.abs(qpos - kpos).astype(F32), -NEG)

        nch = 2 if tk % (2 * MXU_COLS) == 0 else 1
        ck = tk // nch

        def logits(h, c):
            hs = slice(h * HEAD_DIM, (h + 1) * HEAD_DIM)
            return lax.dot_general(k_ref[0, c * ck:(c + 1) * ck, hs], qa_ref[0, :, hs],
                                   (((1,), (1,)), ((), ())), preferred_element_type=F32)

        s_next = [logits(0, c) for c in range(nch)]
        for h in range(N_HEADS_A):
            hx = slice(h * V_ROWS, (h + 1) * V_ROWS)
            s_raw, s_next = s_next, []
            m_prev = m_sc[h:h + 1, :]
            m_new = m_prev
            s = []
            for c in range(nch):
                if h + 1 < N_HEADS_A:
                    s_next.append(logits(h + 1, c))
                s.append(s_raw[c] - ((2.0 ** -(h + 1)) * LOG2E) * dist[c * ck:(c + 1) * ck])
                m_new = jnp.maximum(m_new, jnp.max(s[c], axis=0, keepdims=True))
            acc = jnp.exp2(m_prev - m_new) * acc_sc[hx, :]
            for c in range(nch):
                p = jnp.exp2(s[c] - m_new)
                acc = acc + jnp.dot(vt_ref[0, hx, c * ck:(c + 1) * ck], p.astype(BF16),
                                    preferred_element_type=F32)
            acc_sc[hx, :] = acc
            m_sc[h:h + 1, :] = m_new

    _attend()

    @pl.when(j == nblk - 1)
    def _finalize():
        for h in range(N_HEADS_A):
            hs = slice(h * HEAD_DIM, (h + 1) * HEAD_DIM)
            num = acc_sc[h * V_ROWS:h * V_ROWS + HEAD_DIM, :]
            den = acc_sc[h * V_ROWS + HEAD_DIM:h * V_ROWS + HEAD_DIM + 1, :]
            o_ref[0, :, hs] = (num / den).T.astype(o_ref.dtype)


def _mixer_a(q2, qi, wi, ki2, k_all, v_all, *, past, valid_len, topk, tq, tk):
    b, t, _ = q2.shape
    lp = k_all.shape[1]
    nkb = lp // tk
    assert lp % tk == 0 and t % tq == 0 and tk % LANES == 0 and past % CHUNK == 0
    ki2 = ki2.reshape(b, nkb, tk, ki2.shape[2])
    wit = jnp.swapaxes(wi, 1, 2)
    vt = jnp.swapaxes(v_all, 1, 2).reshape(b, N_HEADS_A, HEAD_DIM, lp)
    vt = jnp.concatenate([vt, jnp.ones((b, N_HEADS_A, V_ROWS - HEAD_DIM, lp), BF16)], axis=2)
    vt = vt.reshape(b, N_HEADS_A * V_ROWS, lp)

    def blocks_needed(i):
        kend = min(-(-(past + (i + 1) * tq) // CHUNK) * CHUNK, valid_len)
        return -(-kend // tk)

    pairs = [(i, j) for i in range(t // tq) for j in range(blocks_needed(i))]
    tile_of = jnp.asarray([p[0] for p in pairs], jnp.int32)
    block_of = jnp.asarray([p[1] for p in pairs], jnp.int32)
    q_map = lambda bb, p, ti, bj: (bb, ti[p], 0)
    kernel = functools.partial(_mixer_a_kernel, tq=tq, tk=tk, past=past, valid_len=valid_len, topk=topk)
    grid_spec = pltpu.PrefetchScalarGridSpec(
        num_scalar_prefetch=2,
        grid=(b, len(pairs)),
        in_specs=[pl.BlockSpec((1, tq, W_A), q_map),
                  pl.BlockSpec((1, tq, N_IDX_HEADS * IDX_DIM), q_map),
                  pl.BlockSpec((1, N_IDX_HEADS, tq), lambda bb, p, ti, bj: (bb, 0, ti[p])),
                  pl.BlockSpec((1, nkb, tk, 2 * IDX_DIM), lambda bb, p, ti, bj: (bb, 0, 0, 0)),
                  pl.BlockSpec((1, tk, W_A), lambda bb, p, ti, bj: (bb, bj[p], 0)),
                  pl.BlockSpec((1, N_HEADS_A * V_ROWS, tk), lambda bb, p, ti, bj: (bb, 0, bj[p]))],
        out_specs=pl.BlockSpec((1, tq, W_A), q_map),
        scratch_shapes=[pltpu.VMEM((nkb, tk, tq), jnp.int32),
                        pltpu.VMEM((nkb, tk, tq), jnp.int16),
                        pltpu.VMEM((nkb, tk, tq), jnp.int16),
                        pltpu.VMEM((8, tq), jnp.int32),
                        pltpu.VMEM((N_HEADS_A, tq), F32),
                        pltpu.VMEM((N_HEADS_A * V_ROWS, tq), F32),
                        pltpu.VMEM((N_IDX_HEADS, 4 * IDX_DIM, tq), BF16)])
    return pl.pallas_call(
        kernel,
        grid_spec=grid_spec,
        out_shape=jax.ShapeDtypeStruct((b, t, W_A), BF16),
        compiler_params=_cp("arbitrary", "arbitrary"),
        name="mixer_a",
    )(tile_of, block_of, q2, qi, wit, ki2, k_all, vt)


def _mixer_b_kernel(*refs, nkb, tkb):
    q_ref = refs[0]
    k_refs = refs[1:1 + nkb]
    v_refs = refs[1 + nkb:1 + 2 * nkb]
    bias_ref = refs[1 + 2 * nkb]
    o_ref = refs[2 + 2 * nkb]
    def logits(h, c):
        hs = slice(h * HEAD_DIM, (h + 1) * HEAD_DIM)
        return lax.dot_general(k_refs[c][0, :, hs], q_ref[0, :, hs], (((1,), (1,)), ((), ())),
                               preferred_element_type=F32)

    s_next = [logits(0, c) for c in range(nkb)]
    for h in range(N_HEADS_B):
        hs = slice(h * HEAD_DIM, (h + 1) * HEAD_DIM)
        s_raw, s_next = s_next, []
        s = []
        m = None
        for c in range(nkb):
            if h + 1 < N_HEADS_B:
                s_next.append(logits(h + 1, c))
            s.append(s_raw[c] * (ATTN_SCALE * LOG2E) + bias_ref[0, h, c * tkb:(c + 1) * tkb, :])
            mc = jnp.max(s[c], axis=0, keepdims=True)
            m = mc if m is None else jnp.maximum(m, mc)
        l = None
        pv = None
        for c, vr in enumerate(v_refs):
            p = jnp.exp2(s[c] - m)
            lc = jnp.sum(p, axis=0, keepdims=True)
            t = lax.dot_general(vr[0, :, hs], p.astype(BF16), (((0,), (0,)), ((), ())),
                                preferred_element_type=F32)
            l = lc if l is None else l + lc
            pv = t if pv is None else pv + t
        o_ref[0, :, hs] = (pv / l).T.astype(o_ref.dtype)


def _band_bias(rel_bias, tq, wk, off, lows, hi):
    nh = rel_bias.shape[0]
    n = wk + tq
    e = np.concatenate([np.arange(wk + 1), np.arange(-(tq - 1), 0)])
    f = rel_bias[:, np.clip(off - e, -REL_CLIP, REL_CLIP) + REL_CLIP].astype(F32) * LOG2E
    bias = jnp.tile(f, (1, tq))[:, :tq * (n - 1)].reshape(nh, tq, n - 1)[:, :, :wk]
    bias = jnp.swapaxes(bias, 1, 2)
    c = np.arange(wk)[:, None]
    r = np.arange(tq)[None, :]
    dq = r // CHUNK
    dk = np.floor_divide(c - off, CHUNK)
    ok = (dk <= dq) & (dk >= dq - BAND_CHUNKS) & (c < hi)
    tiles = [jnp.where((ok & (c >= lo))[None], bias, NEG) for lo in lows]
    return jnp.stack(tiles)


def _mixer_b(q2, kv, bias, *, tq, tkb, nkb, back):
    b, t, _ = q2.shape
    nvar = bias.shape[0]
    wk = nkb * tkb
    kmaps = [functools.partial(lambda bb, i, p, col: (bb, jnp.maximum(i - back + p, 0), col), p=p, col=0)
             for p in range(nkb)]
    vmaps = [functools.partial(lambda bb, i, p, col: (bb, jnp.maximum(i - back + p, 0), col), p=p, col=1)
             for p in range(nkb)]
    kernel = functools.partial(_mixer_b_kernel, nkb=nkb, tkb=tkb)
    return pl.pallas_call(
        kernel,
        grid=(b, t // tq),
        in_specs=([pl.BlockSpec((1, tq, W_B), lambda bb, i: (bb, i, 0))]
                  + [pl.BlockSpec((1, tkb, W_B), m) for m in kmaps]
                  + [pl.BlockSpec((1, tkb, W_B), m) for m in vmaps]
                  + [pl.BlockSpec((1, N_HEADS_B, wk, tq), lambda bb, i: (jnp.minimum(i, nvar - 1), 0, 0, 0))]),
        out_specs=pl.BlockSpec((1, tq, W_B), lambda bb, i: (bb, i, 0)),
        out_shape=jax.ShapeDtypeStruct((b, t, W_B), BF16),
        compiler_params=_cp("arbitrary", "arbitrary"),
        name="mixer_b",
    )(q2, *([kv] * (2 * nkb)), bias)


def _merge_kernel(h_ref, oa_ref, ob_ref, pa_ref, pb_ref, wga_ref, wgb_ref, bga_ref, bgb_ref, o_ref):
    h = h_ref[...]
    ga = jax.nn.sigmoid(jnp.dot(h, wga_ref[...], preferred_element_type=F32) + bga_ref[...])
    gb = jax.nn.sigmoid(jnp.dot(h, wgb_ref[...], preferred_element_type=F32) + bgb_ref[...])
    ya = jnp.dot(oa_ref[...], pa_ref[...], preferred_element_type=F32)
    yb = jnp.dot(ob_ref[...], pb_ref[...], preferred_element_type=F32)
    o_ref[...] = (ga * ya + gb * yb).astype(o_ref.dtype)


def _merge(h, oa, ob, pa, pb, wg, bg):
    m, d = h.shape
    tm = _pick(m, 512, 16)
    tn = _pick(d, 512, LANES)
    nb = d // tn
    bg2 = bg.reshape(1, 2 * d)
    return pl.pallas_call(
        _merge_kernel,
        grid=(m // tm, nb),
        in_specs=[pl.BlockSpec((tm, d), lambda i, j: (i, 0)),
                  pl.BlockSpec((tm, W_A), lambda i, j: (i, 0)),
                  pl.BlockSpec((tm, W_B), lambda i, j: (i, 0)),
                  pl.BlockSpec((W_A, tn), lambda i, j: (0, j)),
                  pl.BlockSpec((W_B, tn), lambda i, j: (0, j)),
                  pl.BlockSpec((d, tn), lambda i, j: (0, j)),
                  pl.BlockSpec((d, tn), lambda i, j: (0, j + nb)),
                  pl.BlockSpec((1, tn), lambda i, j: (0, j)),
                  pl.BlockSpec((1, tn), lambda i, j: (0, j + nb))],
        out_specs=pl.BlockSpec((tm, tn), lambda i, j: (i, j)),
        out_shape=jax.ShapeDtypeStruct((m, d), BF16),
        compiler_params=_cp("arbitrary", "arbitrary"),
        name="merge",
    )(h, oa, ob, pa, pb, wg, wg, bg2, bg2)


def _outproj_kernel(mg_ref, x_ref, w_ref, mod_ref, g_ref, x1_ref, h2_ref):
    o = jnp.dot(mg_ref[0], w_ref[...], preferred_element_type=F32)
    x1 = x_ref[0] + mod_ref[0, 2:3, :] * o
    x1_ref[0] = x1
    h2_ref[0] = _rms_mod(x1, g_ref[...], mod_ref[0, 3:4, :], mod_ref[0, 4:5, :]).astype(h2_ref.dtype)


def _outproj(merged, x, w_out, mod, g2):
    b, t, d = x.shape
    tm = _pick(t, 256, 16)
    row = lambda i, j: (i, j, 0)
    return pl.pallas_call(
        _outproj_kernel,
        grid=(b, t // tm),
        in_specs=[pl.BlockSpec((1, tm, d), row),
                  pl.BlockSpec((1, tm, d), row),
                  pl.BlockSpec((d, d), lambda i, j: (0, 0)),
                  pl.BlockSpec((1, 6, d), lambda i, j: (i, 0, 0)),
                  pl.BlockSpec((1, d), lambda i, j: (0, 0))],
        out_specs=[pl.BlockSpec((1, tm, d), row), pl.BlockSpec((1, tm, d), row)],
        out_shape=[jax.ShapeDtypeStruct((b, t, d), F32), jax.ShapeDtypeStruct((b, t, d), BF16)],
        compiler_params=_cp("arbitrary", "arbitrary"),
        name="outproj",
    )(merged, x, w_out, mod, g2.reshape(1, d))


def _ffn_kernel(h2_ref, x1_ref, mod_ref, wa_ref, wb_ref, wd_ref, wc_ref, bc_ref, prev_ref, fg_ref,
                y_ref, st_ref, abuf, carry, acc_sc, *, tm, nf, weights_outer):
    if weights_outer:
        f, slot, m = pl.program_id(0), pl.program_id(1), pl.program_id(2)
    else:
        slot, m, f = 0, pl.program_id(1), pl.program_id(2)

    @pl.when(f == 0)
    def _():
        acc_sc[slot] = jnp.zeros(acc_sc.shape[1:], F32)

    h2 = h2_ref[0]
    prev = jnp.where(m == 0, prev_ref[0], carry[f])
    tf = wa_ref.shape[1]
    nsub = 2 if tf % (2 * MXU_COLS) == 0 else 1
    tfs = tf // nsub

    def up(c):
        cols = slice(c * tfs, (c + 1) * tfs)
        return (jnp.dot(h2, wa_ref[:, cols], preferred_element_type=F32),
                jnp.dot(h2, wb_ref[:, cols], preferred_element_type=F32))

    nxt = up(0)
    for c in range(nsub):
        cols = slice(c * tfs, (c + 1) * tfs)
        a, gate = nxt
        if c + 1 < nsub:
            nxt = up(c + 1)
        abuf[6:8, cols] = prev[:, cols]
        abuf[8:8 + tm, cols] = a
        tail = a[tm - 2:tm, :]
        carry[f, :, cols] = tail
        st_ref[0, 0, :, cols] = tail
        conv = (bc_ref[:, cols] + wc_ref[0:1, cols] * abuf[6:6 + tm, cols]
                + wc_ref[1:2, cols] * abuf[7:7 + tm, cols] + wc_ref[2:3, cols] * a)
        u = (conv * jax.nn.sigmoid(conv) * gate).astype(BF16)
        acc_sc[slot] += jnp.dot(u, wd_ref[cols, :], preferred_element_type=F32)

    @pl.when(f == nf - 1)
    def _():
        x2 = x1_ref[0] + mod_ref[0, 5:6, :] * acc_sc[slot]
        y_ref[0] = x2 * lax.rsqrt(jnp.mean(x2 * x2, axis=-1, keepdims=True) + EPS) * fg_ref[...]


def _ffn(h2, x1, mod, wa, wb, wd, wconv, bconv, prev, final_g):
    b, t, d = x1.shape
    dff = wa.shape[1]
    tm = _pick(t, 512, 16)
    tf = _pick(dff, 512, LANES)
    nf = dff // tf
    nm = t // tm
    weights_outer = nm == 1 and b * tm * d * 4 <= FFN_ACC_BYTES
    if weights_outer:
        grid = (nf, b, nm)
        spec = lambda shape, fn: pl.BlockSpec(shape, lambda k, i, j: fn(i, j, k))
        row = lambda i, j, k: (jnp.where(k == nf - 1, i, 0), jnp.where(k == nf - 1, j, 0), 0)
    else:
        grid = (b, nm, nf)
        spec = lambda shape, fn: pl.BlockSpec(shape, fn)
        row = lambda i, j, k: (i, j, 0)
    kernel = functools.partial(_ffn_kernel, tm=tm, nf=nf, weights_outer=weights_outer)
    return pl.pallas_call(
        kernel,
        grid=grid,
        in_specs=[spec((1, tm, d), lambda i, j, k: (i, j, 0)),
                  spec((1, tm, d), row),
                  spec((1, 6, d), lambda i, j, k: (i, 0, 0)),
                  spec((d, tf), lambda i, j, k: (0, k)),
                  spec((d, tf), lambda i, j, k: (0, k)),
                  spec((tf, d), lambda i, j, k: (k, 0)),
                  spec((CONV_W, tf), lambda i, j, k: (0, k)),
                  spec((1, tf), lambda i, j, k: (0, k)),
                  spec((1, CONV_W - 1, tf), lambda i, j, k: (i, 0, k)),
                  spec((1, d), lambda i, j, k: (0, 0))],
        out_specs=[spec((1, tm, d), row),
                   spec((1, 1, CONV_W - 1, tf), lambda i, j, k: (i, j, 0, k))],
        out_shape=[jax.ShapeDtypeStruct((b, t, d), F32),
                   jax.ShapeDtypeStruct((b, nm, CONV_W - 1, dff), F32)],
        scratch_shapes=[pltpu.VMEM((tm + 8, tf), F32),
                        pltpu.VMEM((nf, CONV_W - 1, tf), F32),
                        pltpu.VMEM((b if weights_outer else 1, tm, d), F32)],
        compiler_params=_cp("arbitrary", "arbitrary", "arbitrary"),
        name="ffn",
    )(h2, x1, mod, wa, wb, wd, wconv, bconv.reshape(1, dff), prev, final_g.reshape(1, d))


def _prep_weights(w_in, w_gate, w_proj_a, w_proj_b, w_out, w_up, w_down):
    o = np.cumsum((0, W_A, W_A, W_A, N_IDX_HEADS * IDX_DIM, IDX_DIM, N_IDX_HEADS, W_B, W_B, W_B))
    col = lambda a, b: w_in[:, int(o[a]):int(o[b])]
    d = w_in.shape[0]
    dff = w_down.shape[0]
    pad = jnp.zeros((d, MXU_COLS - 2 * IDX_DIM - N_IDX_HEADS), w_in.dtype)
    qi_hi, qi_lo = _split_bf16(col(3, 4))
    kiwi_hi, kiwi_lo = _split_bf16(jnp.concatenate([col(4, 5), col(4, 5), col(5, 6), pad], axis=1))
    return dict(
        qa=col(0, 1).astype(BF16),
        qb=col(6, 7).astype(BF16),
        ka=col(1, 2).astype(BF16),
        va=col(2, 3).astype(BF16),
        qi_hi=qi_hi, qi_lo=qi_lo, kiwi_hi=kiwi_hi, kiwi_lo=kiwi_lo,
        kvb=col(7, 9).astype(BF16),
        gate=w_gate.astype(BF16),
        pa=w_proj_a.astype(BF16),
        pb=w_proj_b.astype(BF16),
        out=w_out.astype(BF16),
        up_a=w_up[:, :dff].astype(BF16),
        up_b=w_up[:, dff:].astype(BF16),
        down=w_down.astype(BF16),
    )


def _trunk_layer(x, mod, cache, wts, norm1_g, rel_bias, b_gate, norm2_g, w_conv, b_conv, final_g):
    b, t, d = x.shape
    dff = wts["down"].shape[0]
    h, h_lo = _norm_mod(x, norm1_g, mod)
    h2d = h.reshape(b * t, d)
    hl2d = h_lo.reshape(b * t, d)
    (qa,) = _mm(h2d, wts["qa"], (BF16,), scale=ATTN_SCALE * LOG2E)
    (qb,) = _mm(h2d, wts["qb"], (BF16,))
    ka32, ka16 = _mm(h2d, wts["ka"], (F32, BF16), heads_out=(0,))
    va32, va16 = _mm(h2d, wts["va"], (F32, BF16), heads_out=(0,))
    qi = _mm_split(h2d, hl2d, wts["qi_hi"], wts["qi_lo"]).reshape(b, t, N_IDX_HEADS * IDX_DIM)
    kiwi = _mm_split(h2d, hl2d, wts["kiwi_hi"], wts["kiwi_lo"])
    (kvb16,) = _mm(h2d, wts["kvb"], (BF16,))
    qa = qa.reshape(b, t, W_A)
    qb = qb.reshape(b, t, W_B)
    kiwi = kiwi.reshape(b, t, MXU_COLS)
    ki = kiwi[:, :, :IDX_DIM]
    wi = kiwi[:, :, 2 * IDX_DIM:2 * IDX_DIM + N_IDX_HEADS]
    ka16 = ka16.reshape(b, t, W_A)
    va16 = va16.reshape(b, t, W_A)
    kvb16 = kvb16.reshape(b, t, 2 * W_B)

    if cache is None:
        rows_tail = min(BAND_PAST, t)
        tq_a = _pick(t, 256, CHUNK)
        tk_a = _pick(t, 512, LANES)
        o_a = _mixer_a(qa, qi, wi, kiwi, ka16, va16, past=0, valid_len=t, topk=min(TOPK_MAX, t // 4),
                       tq=tq_a, tk=tk_a)
        tq_b = _pick(t, 256, CHUNK)
        assert BAND_PAST % tq_b == 0
        back = BAND_PAST // tq_b
        lows = [BAND_PAST - v * tq_b for v in range(back + 1)]
        bias = _band_bias(rel_bias, tq_b, BAND_PAST + tq_b, BAND_PAST, lows, BAND_PAST + tq_b)
        o_b = _mixer_b(qb, kvb16, bias, tq=tq_b, tkb=tq_b, nkb=back + 1, back=back)
        conv_prev = jnp.zeros((b, CONV_W - 1, dff), F32)
    else:
        ck, cv, cki, cbk, cbv, conv_prev = cache
        past = ck.shape[1]
        rows_tail = t
        l_valid = past + t
        lp = -(-l_valid // LANES) * LANES
        tk_a = lp
        padk = jnp.zeros((b, lp - l_valid, W_A), BF16)
        k_all = jnp.concatenate([ck.reshape(b, past, W_A).astype(BF16), ka16, padk], axis=1)
        v_all = jnp.concatenate([cv.reshape(b, past, W_A).astype(BF16), va16, padk], axis=1)
        ki_all = jnp.concatenate([cki, ki, jnp.zeros((b, lp - l_valid, IDX_DIM), F32)], axis=1)
        ki_all = jnp.concatenate([ki_all, ki_all], axis=2)
        tq_a = -(-t // LANES) * LANES
        padq = lambda a: jnp.pad(a, ((0, 0), (0, tq_a - t), (0, 0)))
        o_a = _mixer_a(padq(qa), padq(qi), padq(wi), ki_all, k_all, v_all, past=past, valid_len=l_valid,
                       topk=min(TOPK_MAX, l_valid // 4), tq=tq_a, tk=tk_a)[:, :t]
        rows = cbk.shape[1]
        assert past % CHUNK == 0 and rows % CHUNK == 0
        lb_valid = rows + t
        lb = -(-lb_valid // LANES) * LANES
        kv_cache = jnp.concatenate([cbk.reshape(b, rows, W_B), cbv.reshape(b, rows, W_B)], axis=2).astype(BF16)
        kv_all = jnp.concatenate([kv_cache, kvb16, jnp.zeros((b, lb - lb_valid, 2 * W_B), BF16)], axis=1)
        bias = _band_bias(rel_bias, tq_a, lb, rows, [0], lb_valid)
        o_b = _mixer_b(padq(qb), kv_all, bias, tq=tq_a, tkb=lb, nkb=1, back=0)[:, :t]

    (tail32,) = _mm(h[:, t - rows_tail:].reshape(b * rows_tail, d), wts["kvb"], (F32,))
    tail32 = tail32.reshape(b, rows_tail, 2, N_HEADS_B, HEAD_DIM)
    new_bk, new_bv = tail32[:, :, 0], tail32[:, :, 1]

    merged = _merge(h2d, o_a.reshape(b * t, W_A), o_b.reshape(b * t, W_B), wts["pa"], wts["pb"],
                    wts["gate"], b_gate)
    x1, h2 = _outproj(merged.reshape(b, t, d), x, wts["out"], mod, norm2_g)
    y, tails = _ffn(h2, x1, mod, wts["up_a"], wts["up_b"], wts["down"], w_conv, b_conv,
                    conv_prev, final_g)
    conv_state = tails[:, -1]
    state = (ka32.reshape(b, t, N_HEADS_A, HEAD_DIM), va32.reshape(b, t, N_HEADS_A, HEAD_DIM), ki,
             new_bk, new_bv, conv_state)
    return y, state


def kernel(x_prompt, x_sample, cache_a_k, cache_a_v, cache_idx_k, cache_b_k, cache_b_v, state_ffn_conv,
           c_prompt, c_sample, w_ada, b_ada, norm1_g, w_in, rel_bias, w_gate, b_gate, w_proj_a, w_proj_b,
           w_out, norm2_g, w_up, w_conv, b_conv, w_down, final_g):
    depth = w_ada.shape[0]
    assert depth == 1, "the fused final RMSNorm assumes a single layer"
    d = x_prompt.shape[-1]
    nb_p = x_prompt.shape[0]
    c_all = jnp.concatenate([c_prompt, c_sample], axis=0)
    xp, xs = x_prompt, x_sample
    states_p, states_s = [], []
    for l in range(depth):
        wts = _prep_weights(w_in[l], w_gate[l], w_proj_a[l], w_proj_b[l], w_out[l], w_up[l], w_down[l])
        mod = _ada(c_all, w_ada[l], b_ada[l]).reshape(c_all.shape[0], 6, d)
        args = (wts, norm1_g[l], rel_bias[l], b_gate[l], norm2_g[l], w_conv[l], b_conv[l], final_g)
        xp, st_p = _trunk_layer(xp, mod[:nb_p], None, *args)
        states_p.append(st_p)
        cache_l = (cache_a_k[l], cache_a_v[l], cache_idx_k[l], cache_b_k[l], cache_b_v[l], state_ffn_conv[l])
        xs, st_s = _trunk_layer(xs, mod[nb_p:], cache_l, *args)
        states_s.append(st_s)
    sp = [t[0][None] for t in zip(*states_p)]
    ss = [t[0][None] for t in zip(*states_s)]
    return (xp, xs, *sp, *ss)
```

```python
import functools

import jax
import jax.numpy as jnp
import numpy as np
from jax import lax
from jax.experimental import pallas as pl
from jax.experimental.pallas import tpu as pltpu

F32 = jnp.float32
BF16 = jnp.bfloat16

CHUNK = 64
CHUNK_SHIFT = 6
HEAD_DIM = 128
N_HEADS_A = 8
N_IDX_HEADS = 16
IDX_DIM = 64
TOPK_MAX = 256
N_HEADS_B = 8
BAND_CHUNKS = 8
BAND_PAST = BAND_CHUNKS * CHUNK
REL_CLIP = 2 * CHUNK
CONV_W = 3
EPS = 1e-6
NEG = -1e30
W_A = N_HEADS_A * HEAD_DIM
W_B = N_HEADS_B * HEAD_DIM
ATTN_SCALE = HEAD_DIM ** -0.5
LOG2E = 1.4426950408889634
POS_UNBOUNDED = 2 ** 30
V_ROWS = HEAD_DIM + 16
HALF_BITS = 16
HALF_MASK = 0xFFFF
HALF_BIAS = 1 << (HALF_BITS - 1)
LANES = 128
MXU_COLS = 256
VMEM_LIMIT = 56 * 1024 * 1024
FFN_ACC_BYTES = 8 * 1024 * 1024


def _cp(*sem):
    return pltpu.CompilerParams(dimension_semantics=sem, vmem_limit_bytes=VMEM_LIMIT)


def _pick(dim, pref, align):
    t = min(pref, dim)
    t -= t % align
    while t >= align:
        if dim % t == 0:
            return t
        t -= align
    return dim


def _ada_kernel(c_ref, w_ref, b_ref, o_ref):
    c = c_ref[...]
    s = c * jax.nn.sigmoid(c)
    o_ref[...] = jnp.dot(s, w_ref[...], preferred_element_type=F32,
                         precision=lax.Precision.HIGHEST) + b_ref[...]


def _ada(c, w, b):
    r, d = c.shape
    n = w.shape[1]
    tn = _pick(n, 1024, LANES)
    return pl.pallas_call(
        _ada_kernel,
        grid=(n // tn,),
        in_specs=[pl.BlockSpec((r, d), lambda j: (0, 0)),
                  pl.BlockSpec((d, tn), lambda j: (0, j)),
                  pl.BlockSpec((1, tn), lambda j: (0, j))],
        out_specs=pl.BlockSpec((r, tn), lambda j: (0, j)),
        out_shape=jax.ShapeDtypeStruct((r, n), F32),
        compiler_params=_cp("arbitrary"),
        name="ada",
    )(c, w, b.reshape(1, n))


def _rms_mod(x, g, shift, scale):
    y = x * lax.rsqrt(jnp.mean(x * x, axis=-1, keepdims=True) + EPS) * g
    return y * (1.0 + scale) + shift


def _split_bf16(x):
    hi = x.astype(BF16)
    return hi, (x - hi.astype(F32)).astype(BF16)


def _norm_mod_kernel(x_ref, g_ref, mod_ref, hi_ref, lo_ref):
    h = _rms_mod(x_ref[0], g_ref[...], mod_ref[0, 0:1, :], mod_ref[0, 1:2, :])
    hi, lo = _split_bf16(h)
    hi_ref[0] = hi
    lo_ref[0] = lo


def _norm_mod(x, g, mod):
    b, t, d = x.shape
    tm = _pick(t, 512, 16)
    row = lambda i, j: (i, j, 0)
    return pl.pallas_call(
        _norm_mod_kernel,
        grid=(b, t // tm),
        in_specs=[pl.BlockSpec((1, tm, d), row),
                  pl.BlockSpec((1, d), lambda i, j: (0, 0)),
                  pl.BlockSpec((1, 6, d), lambda i, j: (i, 0, 0))],
        out_specs=[pl.BlockSpec((1, tm, d), row), pl.BlockSpec((1, tm, d), row)],
        out_shape=[jax.ShapeDtypeStruct((b, t, d), BF16), jax.ShapeDtypeStruct((b, t, d), BF16)],
        compiler_params=_cp("arbitrary", "arbitrary"),
        name="norm_mod",
    )(x, g.reshape(1, d), mod)


def _mm_kernel(a_ref, w_ref, *o_refs, scale):
    acc = jnp.dot(a_ref[...], w_ref[...], preferred_element_type=F32)
    if scale is not None:
        acc = acc * scale
    for o_ref in o_refs:
        o_ref[...] = acc.astype(o_ref.dtype).reshape(o_ref.shape)


def _mm(a, w, dtypes, scale=None, heads_out=()):
    m, k = a.shape
    n = w.shape[1]
    tm = _pick(m, 1024, 16)
    tn = _pick(n, 1024, LANES)
    nh = tn // HEAD_DIM
    specs = [pl.BlockSpec((tm, nh, HEAD_DIM), lambda i, j: (i, j, 0)) if o in heads_out
             else pl.BlockSpec((tm, tn), lambda i, j: (i, j)) for o in range(len(dtypes))]
    shapes = [jax.ShapeDtypeStruct((m, n // HEAD_DIM, HEAD_DIM) if o in heads_out else (m, n), dt)
              for o, dt in enumerate(dtypes)]
    outs = pl.pallas_call(
        functools.partial(_mm_kernel, scale=scale),
        grid=(m // tm, n // tn),
        in_specs=[pl.BlockSpec((tm, k), lambda i, j: (i, 0)),
                  pl.BlockSpec((k, tn), lambda i, j: (0, j))],
        out_specs=specs,
        out_shape=shapes,
        compiler_params=_cp("arbitrary", "arbitrary"),
        name="proj",
    )(a, w)
    return outs


def _mm_split_kernel(ah_ref, al_ref, wh_ref, wl_ref, o_ref):
    ah = ah_ref[...]
    acc = jnp.dot(ah, wh_ref[...], preferred_element_type=F32)
    acc = acc + jnp.dot(al_ref[...], wh_ref[...], preferred_element_type=F32)
    acc = acc + jnp.dot(ah, wl_ref[...], preferred_element_type=F32)
    o_ref[...] = acc


def _mm_split(a_hi, a_lo, w_hi, w_lo):
    m, k = a_hi.shape
    n = w_hi.shape[1]
    tm = _pick(m, 1024, 16)
    tn = _pick(n, 1024, LANES)
    a_spec = pl.BlockSpec((tm, k), lambda i, j: (i, 0))
    w_spec = pl.BlockSpec((k, tn), lambda i, j: (0, j))
    return pl.pallas_call(
        _mm_split_kernel,
        grid=(m // tm, n // tn),
        in_specs=[a_spec, a_spec, w_spec, w_spec],
        out_specs=pl.BlockSpec((tm, tn), lambda i, j: (i, j)),
        out_shape=jax.ShapeDtypeStruct((m, n), F32),
        compiler_params=_cp("arbitrary", "arbitrary"),
        name="proj_split",
    )(a_hi, a_lo, w_hi, w_lo)


def _mixer_a_kernel(tile_ref, block_ref, qa_ref, qi_ref, wit_ref, ki2_ref, k_ref, vt_ref, o_ref,
                    key_sc, hi_sc, lo_sc, thr_sc, cut_sc, tie_sc, dist_sc, m_sc, acc_sc, qx_sc,
                    *, tq, tk, past, valid_len, topk):
    i = tile_ref[pl.program_id(1)]
    j = block_ref[pl.program_id(1)]
    q0 = past + i * tq
    kend = jnp.minimum(((q0 + tq + CHUNK - 1) // CHUNK) * CHUNK, valid_len)
    nblk = (kend + tk - 1) // tk
    slab16 = 16

    def admissible(jb, row0=0, rows=tk):
        kpos = jb * tk + row0 + lax.broadcasted_iota(jnp.int32, (rows, tq), 0)
        qpos = q0 + lax.broadcasted_iota(jnp.int32, (rows, tq), 1)
        adm = (jnp.right_shift(kpos, CHUNK_SHIFT) <= jnp.right_shift(qpos, CHUNK_SHIFT)) & (kpos < valid_len)
        return adm, qpos, kpos

    @pl.when(j == 0)
    def _index_and_threshold():
        qt = qi_ref[0].T
        q_hi, q_lo = _split_bf16(qt)
        for h in range(N_IDX_HEADS):
            rows = slice(h * IDX_DIM, (h + 1) * IDX_DIM)
            qx_sc[h, 0 * IDX_DIM:1 * IDX_DIM, :] = q_hi[rows]
            qx_sc[h, 1 * IDX_DIM:2 * IDX_DIM, :] = q_hi[rows]
            qx_sc[h, 2 * IDX_DIM:3 * IDX_DIM, :] = q_lo[rows]
            qx_sc[h, 3 * IDX_DIM:4 * IDX_DIM, :] = q_lo[rows]


        nsplit = 2 if tk % (2 * MXU_COLS) == 0 else 1
        ts = tk // nsplit
        first = lax.broadcasted_iota(jnp.int32, (ts, LANES), 1) < IDX_DIM

        def score_block(jb, carry):
            for r in range(nsplit):
                rows = slice(r * ts, (r + 1) * ts)
                dup = ki2_ref[0, jb, rows, :]
                kk = jnp.where(first, dup, dup - dup.astype(BF16).astype(F32)).astype(BF16)
                kx = jnp.concatenate([kk, kk], axis=1)
                acc = jnp.zeros((ts, tq), F32)
                for h in range(N_IDX_HEADS):
                    rel = jnp.dot(kx, qx_sc[h], preferred_element_type=F32)
                    acc = acc + wit_ref[0, h:h + 1, :] * jnp.maximum(rel, 0.0)
                acc = jnp.where(admissible(jb, r * ts, ts)[0], acc, NEG)
                bits = pltpu.bitcast(acc, jnp.int32)
                key = bits ^ (jnp.right_shift(bits, 31) & 0x7FFFFFFF)
                key_sc[jb, rows, :] = key
                hi_sc[jb, rows, :] = jnp.right_shift(key, HALF_BITS).astype(jnp.int16)
                lo_sc[jb, rows, :] = ((key & HALF_MASK) - HALF_BIAS).astype(jnp.int16)
            return carry

        lax.fori_loop(0, nblk, score_block, 0)

        def count_ge(src_sc, cand):
            cand16 = cand.astype(jnp.int16)

            def count_block(jb, cnt):
                part = [None] * 4
                for r in range(tk // slab16):
                    rows = slice(r * slab16, (r + 1) * slab16)
                    hit = jnp.where(src_sc[jb, rows, :] >= cand16, jnp.int16(1), jnp.int16(0))
                    part[r % 4] = hit if part[r % 4] is None else part[r % 4] + hit
                return cnt + ((part[0] + part[1]) + (part[2] + part[3]))

            cnt = lax.fori_loop(0, nblk, count_block, jnp.zeros((slab16, tq), jnp.int16))
            return jnp.sum(cnt.astype(jnp.int32), axis=0, keepdims=True)

        def kth_largest16(src_sc, want):
            def bit_step(b, t_u):
                cand_u = t_u | lax.shift_left(jnp.int32(1), HALF_BITS - 1 - b)
                return jnp.where(count_ge(src_sc, cand_u - HALF_BIAS) >= want, cand_u, t_u)
            t_u = lax.fori_loop(0, HALF_BITS, bit_step, jnp.zeros((slab16, tq), jnp.int32))
            return t_u - HALF_BIAS

        t_hi = kth_largest16(hi_sc, topk)
        above = jnp.where(t_hi == HALF_BIAS - 1, 0, count_ge(hi_sc, jnp.minimum(t_hi + 1, HALF_BIAS - 1)))
        t_hi16 = t_hi.astype(jnp.int16)

        def keep_group(jb, carry):
            for r in range(tk // slab16):
                rows = slice(r * slab16, (r + 1) * slab16)
                lo_sc[jb, rows, :] = jnp.where(hi_sc[jb, rows, :] == t_hi16, lo_sc[jb, rows, :],
                                               jnp.int16(-HALF_BIAS))
            return carry

        lax.fori_loop(0, nblk, keep_group, 0)
        t_lo = kth_largest16(lo_sc, topk - above)
        thr = (t_hi * (2 * HALF_BIAS) + (t_lo + HALF_BIAS))[0:1, :]
        thr_sc[...] = jnp.broadcast_to(thr, thr_sc.shape)

        def count_keys(pred):
            def count_block(jb, cnt):
                hit = pred(key_sc[jb], admissible(jb)[2])
                return cnt + jnp.sum(jnp.where(hit, 1, 0), axis=0, keepdims=True)
            return lax.fori_loop(0, nblk, count_block, jnp.zeros((1, tq), jnp.int32))

        surplus = jnp.max(count_keys(lambda key, kpos: key >= thr)) - topk
        tie_sc[0] = surplus
        cut_sc[...] = jnp.full(cut_sc.shape, POS_UNBOUNDED, jnp.int32)

        @pl.when(surplus > 0)
        def _break_ties():
            keep = topk - count_keys(lambda key, kpos: key > thr)
            pos_bits = (key_sc.shape[0] * tk).bit_length()

            def bit_step(b, x):
                cand = x | lax.shift_left(jnp.int32(1), pos_bits - 1 - b)
                below = count_keys(lambda key, kpos: (key == thr) & (kpos < cand))
                return jnp.where(below <= keep, cand, x)

            x = lax.fori_loop(0, pos_bits, bit_step, jnp.zeros((1, tq), jnp.int32))
            cut_sc[...] = jnp.broadcast_to(x, cut_sc.shape)

        m_sc[...] = jnp.full(m_sc.shape, -jnp.inf, F32)
        acc_sc[...] = jnp.zeros(acc_sc.shape, F32)

    def _attend():
        def masked_distance(select):
            adm, qpos, kpos = admissible(j)
            key = key_sc[j]
            dist_sc[...] = jnp.where(select(key, kpos) & adm, jnp.abs(qpos - kpos).astype(F32), -NEG)

        @pl.when(tie_sc[0] <= 0)
        def _():
            masked_distance(lambda key, kpos: key >= thr_sc[0:1, :])

        @pl.when(tie_sc[0] > 0)
        def _():
            masked_distance(lambda key, kpos: (key > thr_sc[0:1, :])
                            | ((key == thr_sc[0:1, :]) & (kpos < cut_sc[0:1, :])))

        nch = 2 if tk % (2 * MXU_COLS) == 0 else 1
        ck = tk // nch

        def logits(h, c):
            hs = slice(h * HEAD_DIM, (h + 1) * HEAD_DIM)
            return lax.dot_general(k_ref[0, c * ck:(c + 1) * ck, hs], qa_ref[0, :, hs],
                                   (((1,), (1,)), ((), ())), preferred_element_type=F32)

        s_next = [logits(0, c) for c in range(nch)]
        for h in range(N_HEADS_A):
            hx = slice(h * V_ROWS, (h + 1) * V_ROWS)
            s_raw, s_next = s_next, []
            m_prev = m_sc[h:h + 1, :]
            m_new = m_prev
            s = []
            for c in range(nch):
                if h + 1 < N_HEADS_A:
                    s_next.append(logits(h + 1, c))
                s.append(s_raw[c] - ((2.0 ** -(h + 1)) * LOG2E) * dist_sc[c * ck:(c + 1) * ck, :])
                m_new = jnp.maximum(m_new, jnp.max(s[c], axis=0, keepdims=True))
            acc = jnp.exp2(m_prev - m_new) * acc_sc[hx, :]
            for c in range(nch):
                p = jnp.exp2(s[c] - m_new)
                acc = acc + jnp.dot(vt_ref[0, hx, c * ck:(c + 1) * ck], p.astype(BF16),
                                    preferred_element_type=F32)
            acc_sc[hx, :] = acc
            m_sc[h:h + 1, :] = m_new

    _attend()

    @pl.when(j == nblk - 1)
    def _finalize():
        for h in range(N_HEADS_A):
            hs = slice(h * HEAD_DIM, (h + 1) * HEAD_DIM)
            num = acc_sc[h * V_ROWS:h * V_ROWS + HEAD_DIM, :]
            den = acc_sc[h * V_ROWS + HEAD_DIM:h * V_ROWS + HEAD_DIM + 1, :]
            o_ref[0, :, hs] = (num / den).T.astype(o_ref.dtype)


def _mixer_a(q2, qi, wi, ki2, k_all, v_all, *, past, valid_len, topk, tq, tk):
    b, t, _ = q2.shape
    lp = k_all.shape[1]
    nkb = lp // tk
    assert lp % tk == 0 and t % tq == 0 and tk % LANES == 0 and past % CHUNK == 0
    ki2 = ki2.reshape(b, nkb, tk, ki2.shape[2])
    wit = jnp.swapaxes(wi, 1, 2)
    vt = jnp.swapaxes(v_all, 1, 2).reshape(b, N_HEADS_A, HEAD_DIM, lp)
    vt = jnp.concatenate([vt, jnp.ones((b, N_HEADS_A, V_ROWS - HEAD_DIM, lp), BF16)], axis=2)
    vt = vt.reshape(b, N_HEADS_A * V_ROWS, lp)

    def blocks_needed(i):
        kend = min(-(-(past + (i + 1) * tq) // CHUNK) * CHUNK, valid_len)
        return -(-kend // tk)

    pairs = [(i, j) for i in range(t // tq) for j in range(blocks_needed(i))]
    tile_of = jnp.asarray([p[0] for p in pairs], jnp.int32)
    block_of = jnp.asarray([p[1] for p in pairs], jnp.int32)
    q_map = lambda bb, p, ti, bj: (bb, ti[p], 0)
    kernel = functools.partial(_mixer_a_kernel, tq=tq, tk=tk, past=past, valid_len=valid_len, topk=topk)
    grid_spec = pltpu.PrefetchScalarGridSpec(
        num_scalar_prefetch=2,
        grid=(b, len(pairs)),
        in_specs=[pl.BlockSpec((1, tq, W_A), q_map),
                  pl.BlockSpec((1, tq, N_IDX_HEADS * IDX_DIM), q_map),
                  pl.BlockSpec((1, N_IDX_HEADS, tq), lambda bb, p, ti, bj: (bb, 0, ti[p])),
                  pl.BlockSpec((1, nkb, tk, 2 * IDX_DIM), lambda bb, p, ti, bj: (bb, 0, 0, 0)),
                  pl.BlockSpec((1, tk, W_A), lambda bb, p, ti, bj: (bb, bj[p], 0)),
                  pl.BlockSpec((1, N_HEADS_A * V_ROWS, tk), lambda bb, p, ti, bj: (bb, 0, bj[p]))],
        out_specs=pl.BlockSpec((1, tq, W_A), q_map),
        scratch_shapes=[pltpu.VMEM((nkb, tk, tq), jnp.int32),
                        pltpu.VMEM((nkb, tk, tq), jnp.int16),
                        pltpu.VMEM((nkb, tk, tq), jnp.int16),
                        pltpu.VMEM((8, tq), jnp.int32),
                        pltpu.VMEM((8, tq), jnp.int32),
                        pltpu.SMEM((1,), jnp.int32),
                        pltpu.VMEM((tk, tq), F32),
                        pltpu.VMEM((N_HEADS_A, tq), F32),
                        pltpu.VMEM((N_HEADS_A * V_ROWS, tq), F32),
                        pltpu.VMEM((N_IDX_HEADS, 4 * IDX_DIM, tq), BF16)])
    return pl.pallas_call(
        kernel,
        grid_spec=grid_spec,
        out_shape=jax.ShapeDtypeStruct((b, t, W_A), BF16),
        compiler_params=_cp("arbitrary", "arbitrary"),
        name="mixer_a",
    )(tile_of, block_of, q2, qi, wit, ki2, k_all, vt)


def _mixer_b_kernel(*refs, nkb, tkb):
    q_ref = refs[0]
    k_refs = refs[1:1 + nkb]
    v_refs = refs[1 + nkb:1 + 2 * nkb]
    bias_ref = refs[1 + 2 * nkb]
    o_ref = refs[2 + 2 * nkb]
    def logits(h, c):
        hs = slice(h * HEAD_DIM, (h + 1) * HEAD_DIM)
        return lax.dot_general(k_refs[c][0, :, hs], q_ref[0, :, hs], (((1,), (1,)), ((), ())),
                               preferred_element_type=F32)

    s_next = [logits(0, c) for c in range(nkb)]
    for h in range(N_HEADS_B):
        hs = slice(h * HEAD_DIM, (h + 1) * HEAD_DIM)
        s_raw, s_next = s_next, []
        s = []
        m = None
        for c in range(nkb):
            if h + 1 < N_HEADS_B:
                s_next.append(logits(h + 1, c))
            s.append(s_raw[c] * (ATTN_SCALE * LOG2E) + bias_ref[0, h, c * tkb:(c + 1) * tkb, :])
            mc = jnp.max(s[c], axis=0, keepdims=True)
            m = mc if m is None else jnp.maximum(m, mc)
        l = None
        pv = None
        for c, vr in enumerate(v_refs):
            p = jnp.exp2(s[c] - m)
            lc = jnp.sum(p, axis=0, keepdims=True)
            t = lax.dot_general(vr[0, :, hs], p.astype(BF16), (((0,), (0,)), ((), ())),
                                preferred_element_type=F32)
            l = lc if l is None else l + lc
            pv = t if pv is None else pv + t
        o_ref[0, :, hs] = (pv / l).T.astype(o_ref.dtype)


def _band_bias(rel_bias, tq, wk, off, lows, hi):
    nh = rel_bias.shape[0]
    n = wk + tq
    e = np.concatenate([np.arange(wk + 1), np.arange(-(tq - 1), 0)])
    f = rel_bias[:, np.clip(off - e, -REL_CLIP, REL_CLIP) + REL_CLIP].astype(F32) * LOG2E
    bias = jnp.tile(f, (1, tq))[:, :tq * (n - 1)].reshape(nh, tq, n - 1)[:, :, :wk]
    bias = jnp.swapaxes(bias, 1, 2)
    c = np.arange(wk)[:, None]
    r = np.arange(tq)[None, :]
    dq = r // CHUNK
    dk = np.floor_divide(c - off, CHUNK)
    ok = (dk <= dq) & (dk >= dq - BAND_CHUNKS) & (c < hi)
    tiles = [jnp.where((ok & (c >= lo))[None], bias, NEG) for lo in lows]
    return jnp.stack(tiles)


def _mixer_b(q2, kv, bias, *, tq, tkb, nkb, back):
    b, t, _ = q2.shape
    nvar = bias.shape[0]
    wk = nkb * tkb
    kmaps = [functools.partial(lambda bb, i, p, col: (bb, jnp.maximum(i - back + p, 0), col), p=p, col=0)
             for p in range(nkb)]
    vmaps = [functools.partial(lambda bb, i, p, col: (bb, jnp.maximum(i - back + p, 0), col), p=p, col=1)
             for p in range(nkb)]
    kernel = functools.partial(_mixer_b_kernel, nkb=nkb, tkb=tkb)
    return pl.pallas_call(
        kernel,
        grid=(b, t // tq),
        in_specs=([pl.BlockSpec((1, tq, W_B), lambda bb, i: (bb, i, 0))]
                  + [pl.BlockSpec((1, tkb, W_B), m) for m in kmaps]
                  + [pl.BlockSpec((1, tkb, W_B), m) for m in vmaps]
                  + [pl.BlockSpec((1, N_HEADS_B, wk, tq), lambda bb, i: (jnp.minimum(i, nvar - 1), 0, 0, 0))]),
        out_specs=pl.BlockSpec((1, tq, W_B), lambda bb, i: (bb, i, 0)),
        out_shape=jax.ShapeDtypeStruct((b, t, W_B), BF16),
        compiler_params=_cp("arbitrary", "arbitrary"),
        name="mixer_b",
    )(q2, *([kv] * (2 * nkb)), bias)


def _merge_kernel(h_ref, oa_ref, ob_ref, pa_ref, pb_ref, wga_ref, wgb_ref, bga_ref, bgb_ref, o_ref):
    h = h_ref[...]
    ga = jax.nn.sigmoid(jnp.dot(h, wga_ref[...], preferred_element_type=F32) + bga_ref[...])
    gb = jax.nn.sigmoid(jnp.dot(h, wgb_ref[...], preferred_element_type=F32) + bgb_ref[...])
    ya = jnp.dot(oa_ref[...], pa_ref[...], preferred_element_type=F32)
    yb = jnp.dot(ob_ref[...], pb_ref[...], preferred_element_type=F32)
    o_ref[...] = (ga * ya + gb * yb).astype(o_ref.dtype)


def _merge(h, oa, ob, pa, pb, wg, bg):
    m, d = h.shape
    tm = _pick(m, 512, 16)
    tn = _pick(d, 512, LANES)
    nb = d // tn
    bg2 = bg.reshape(1, 2 * d)
    return pl.pallas_call(
        _merge_kernel,
        grid=(m // tm, nb),
        in_specs=[pl.BlockSpec((tm, d), lambda i, j: (i, 0)),
                  pl.BlockSpec((tm, W_A), lambda i, j: (i, 0)),
                  pl.BlockSpec((tm, W_B), lambda i, j: (i, 0)),
                  pl.BlockSpec((W_A, tn), lambda i, j: (0, j)),
                  pl.BlockSpec((W_B, tn), lambda i, j: (0, j)),
                  pl.BlockSpec((d, tn), lambda i, j: (0, j)),
                  pl.BlockSpec((d, tn), lambda i, j: (0, j + nb)),
                  pl.BlockSpec((1, tn), lambda i, j: (0, j)),
                  pl.BlockSpec((1, tn), lambda i, j: (0, j + nb))],
        out_specs=pl.BlockSpec((tm, tn), lambda i, j: (i, j)),
        out_shape=jax.ShapeDtypeStruct((m, d), BF16),
        compiler_params=_cp("arbitrary", "arbitrary"),
        name="merge",
    )(h, oa, ob, pa, pb, wg, wg, bg2, bg2)


def _outproj_kernel(mg_ref, x_ref, w_ref, mod_ref, g_ref, x1_ref, h2_ref):
    o = jnp.dot(mg_ref[0], w_ref[...], preferred_element_type=F32)
    x1 = x_ref[0] + mod_ref[0, 2:3, :] * o
    x1_ref[0] = x1
    h2_ref[0] = _rms_mod(x1, g_ref[...], mod_ref[0, 3:4, :], mod_ref[0, 4:5, :]).astype(h2_ref.dtype)


def _outproj(merged, x, w_out, mod, g2):
    b, t, d = x.shape
    tm = _pick(t, 256, 16)
    row = lambda i, j: (i, j, 0)
    return pl.pallas_call(
        _outproj_kernel,
        grid=(b, t // tm),
        in_specs=[pl.BlockSpec((1, tm, d), row),
                  pl.BlockSpec((1, tm, d), row),
                  pl.BlockSpec((d, d), lambda i, j: (0, 0)),
                  pl.BlockSpec((1, 6, d), lambda i, j: (i, 0, 0)),
                  pl.BlockSpec((1, d), lambda i, j: (0, 0))],
        out_specs=[pl.BlockSpec((1, tm, d), row), pl.BlockSpec((1, tm, d), row)],
        out_shape=[jax.ShapeDtypeStruct((b, t, d), F32), jax.ShapeDtypeStruct((b, t, d), BF16)],
        compiler_params=_cp("arbitrary", "arbitrary"),
        name="outproj",
    )(merged, x, w_out, mod, g2.reshape(1, d))


def _ffn_kernel(h2_ref, x1_ref, mod_ref, wa_ref, wb_ref, wd_ref, wc_ref, bc_ref, prev_ref, fg_ref,
                y_ref, st_ref, abuf, carry, acc_sc, *, tm, nf, weights_outer):
    if weights_outer:
        f, slot, m = pl.program_id(0), pl.program_id(1), pl.program_id(2)
    else:
        slot, m, f = 0, pl.program_id(1), pl.program_id(2)

    @pl.when(f == 0)
    def _():
        acc_sc[slot] = jnp.zeros(acc_sc.shape[1:], F32)

    h2 = h2_ref[0]
    prev = jnp.where(m == 0, prev_ref[0], carry[f])
    tf = wa_ref.shape[1]
    nsub = 2 if tf % (2 * MXU_COLS) == 0 else 1
    tfs = tf // nsub

    def up(c):
        cols = slice(c * tfs, (c + 1) * tfs)
        return (jnp.dot(h2, wa_ref[:, cols], preferred_element_type=F32),
                jnp.dot(h2, wb_ref[:, cols], preferred_element_type=F32))

    nxt = up(0)
    for c in range(nsub):
        cols = slice(c * tfs, (c + 1) * tfs)
        a, gate = nxt
        if c + 1 < nsub:
            nxt = up(c + 1)
        abuf[6:8, cols] = prev[:, cols]
        abuf[8:8 + tm, cols] = a
        tail = a[tm - 2:tm, :]
        carry[f, :, cols] = tail
        st_ref[0, 0, :, cols] = tail
        conv = (bc_ref[:, cols] + wc_ref[0:1, cols] * abuf[6:6 + tm, cols]
                + wc_ref[1:2, cols] * abuf[7:7 + tm, cols] + wc_ref[2:3, cols] * a)
        u = (conv * jax.nn.sigmoid(conv) * gate).astype(BF16)
        acc_sc[slot] += jnp.dot(u, wd_ref[cols, :], preferred_element_type=F32)

    @pl.when(f == nf - 1)
    def _():
        x2 = x1_ref[0] + mod_ref[0, 5:6, :] * acc_sc[slot]
        y_ref[0] = x2 * lax.rsqrt(jnp.mean(x2 * x2, axis=-1, keepdims=True) + EPS) * fg_ref[...]


def _ffn(h2, x1, mod, wa, wb, wd, wconv, bconv, prev, final_g):
    b, t, d = x1.shape
    dff = wa.shape[1]
    tm = _pick(t, 512, 16)
    tf = _pick(dff, 512, LANES)
    nf = dff // tf
    nm = t // tm
    weights_outer = nm == 1 and b * tm * d * 4 <= FFN_ACC_BYTES
    if weights_outer:
        grid = (nf, b, nm)
        spec = lambda shape, fn: pl.BlockSpec(shape, lambda k, i, j: fn(i, j, k))
        row = lambda i, j, k: (jnp.where(k == nf - 1, i, 0), jnp.where(k == nf - 1, j, 0), 0)
    else:
        grid = (b, nm, nf)
        spec = lambda shape, fn: pl.BlockSpec(shape, fn)
        row = lambda i, j, k: (i, j, 0)
    kernel = functools.partial(_ffn_kernel, tm=tm, nf=nf, weights_outer=weights_outer)
    return pl.pallas_call(
        kernel,
        grid=grid,
        in_specs=[spec((1, tm, d), lambda i, j, k: (i, j, 0)),
                  spec((1, tm, d), row),
                  spec((1, 6, d), lambda i, j, k: (i, 0, 0)),
                  spec((d, tf), lambda i, j, k: (0, k)),
                  spec((d, tf), lambda i, j, k: (0, k)),
                  spec((tf, d), lambda i, j, k: (k, 0)),
                  spec((CONV_W, tf), lambda i, j, k: (0, k)),
                  spec((1, tf), lambda i, j, k: (0, k)),
                  spec((1, CONV_W - 1, tf), lambda i, j, k: (i, 0, k)),
                  spec((1, d), lambda i, j, k: (0, 0))],
        out_specs=[spec((1, tm, d), row),
                   spec((1, 1, CONV_W - 1, tf), lambda i, j, k: (i, j, 0, k))],
        out_shape=[jax.ShapeDtypeStruct((b, t, d), F32),
                   jax.ShapeDtypeStruct((b, nm, CONV_W - 1, dff), F32)],
        scratch_shapes=[pltpu.VMEM((tm + 8, tf), F32),
                        pltpu.VMEM((nf, CONV_W - 1, tf), F32),
                        pltpu.VMEM((b if weights_outer else 1, tm, d), F32)],
        compiler_params=_cp("arbitrary", "arbitrary", "arbitrary"),
        name="ffn",
    )(h2, x1, mod, wa, wb, wd, wconv, bconv.reshape(1, dff), prev, final_g.reshape(1, d))


def _prep_weights(w_in, w_gate, w_proj_a, w_proj_b, w_out, w_up, w_down):
    o = np.cumsum((0, W_A, W_A, W_A, N_IDX_HEADS * IDX_DIM, IDX_DIM, N_IDX_HEADS, W_B, W_B, W_B))
    col = lambda a, b: w_in[:, int(o[a]):int(o[b])]
    d = w_in.shape[0]
    dff = w_down.shape[0]
    pad = jnp.zeros((d, MXU_COLS - 2 * IDX_DIM - N_IDX_HEADS), w_in.dtype)
    qi_hi, qi_lo = _split_bf16(col(3, 4))
    kiwi_hi, kiwi_lo = _split_bf16(jnp.concatenate([col(4, 5), col(4, 5), col(5, 6), pad], axis=1))
    return dict(
        qa=col(0, 1).astype(BF16),
        qb=col(6, 7).astype(BF16),
        ka=col(1, 2).astype(BF16),
        va=col(2, 3).astype(BF16),
        qi_hi=qi_hi, qi_lo=qi_lo, kiwi_hi=kiwi_hi, kiwi_lo=kiwi_lo,
        kvb=col(7, 9).astype(BF16),
        gate=w_gate.astype(BF16),
        pa=w_proj_a.astype(BF16),
        pb=w_proj_b.astype(BF16),
        out=w_out.astype(BF16),
        up_a=w_up[:, :dff].astype(BF16),
        up_b=w_up[:, dff:].astype(BF16),
        down=w_down.astype(BF16),
    )


def _trunk_layer(x, mod, cache, wts, norm1_g, rel_bias, b_gate, norm2_g, w_conv, b_conv, final_g):
    b, t, d = x.shape
    dff = wts["down"].shape[0]
    h, h_lo = _norm_mod(x, norm1_g, mod)
    h2d = h.reshape(b * t, d)
    hl2d = h_lo.reshape(b * t, d)
    (qa,) = _mm(h2d, wts["qa"], (BF16,), scale=ATTN_SCALE * LOG2E)
    (qb,) = _mm(h2d, wts["qb"], (BF16,))
    ka32, ka16 = _mm(h2d, wts["ka"], (F32, BF16), heads_out=(0,))
    va32, va16 = _mm(h2d, wts["va"], (F32, BF16), heads_out=(0,))
    qi = _mm_split(h2d, hl2d, wts["qi_hi"], wts["qi_lo"]).reshape(b, t, N_IDX_HEADS * IDX_DIM)
    kiwi = _mm_split(h2d, hl2d, wts["kiwi_hi"], wts["kiwi_lo"])
    (kvb16,) = _mm(h2d, wts["kvb"], (BF16,))
    qa = qa.reshape(b, t, W_A)
    qb = qb.reshape(b, t, W_B)
    kiwi = kiwi.reshape(b, t, MXU_COLS)
    ki = kiwi[:, :, :IDX_DIM]
    wi = kiwi[:, :, 2 * IDX_DIM:2 * IDX_DIM + N_IDX_HEADS]
    ka16 = ka16.reshape(b, t, W_A)
    va16 = va16.reshape(b, t, W_A)
    kvb16 = kvb16.reshape(b, t, 2 * W_B)

    if cache is None:
        rows_tail = min(BAND_PAST, t)
        tq_a = _pick(t, 256, CHUNK)
        tk_a = _pick(t, 512, LANES)
        o_a = _mixer_a(qa, qi, wi, kiwi, ka16, va16, past=0, valid_len=t, topk=min(TOPK_MAX, t // 4),
                       tq=tq_a, tk=tk_a)
        tq_b = _pick(t, 256, CHUNK)
        assert BAND_PAST % tq_b == 0
        back = BAND_PAST // tq_b
        lows = [BAND_PAST - v * tq_b for v in range(back + 1)]
        bias = _band_bias(rel_bias, tq_b, BAND_PAST + tq_b, BAND_PAST, lows, BAND_PAST + tq_b)
        o_b = _mixer_b(qb, kvb16, bias, tq=tq_b, tkb=tq_b, nkb=back + 1, back=back)
        conv_prev = jnp.zeros((b, CONV_W - 1, dff), F32)
    else:
        ck, cv, cki, cbk, cbv, conv_prev = cache
        past = ck.shape[1]
        rows_tail = t
        l_valid = past + t
        lp = -(-l_valid // LANES) * LANES
        tk_a = lp
        padk = jnp.zeros((b, lp - l_valid, W_A), BF16)
        k_all = jnp.concatenate([ck.reshape(b, past, W_A).astype(BF16), ka16, padk], axis=1)
        v_all = jnp.concatenate([cv.reshape(b, past, W_A).astype(BF16), va16, padk], axis=1)
        ki_all = jnp.concatenate([cki, ki, jnp.zeros((b, lp - l_valid, IDX_DIM), F32)], axis=1)
        ki_all = jnp.concatenate([ki_all, ki_all], axis=2)
        tq_a = -(-t // LANES) * LANES
        padq = lambda a: jnp.pad(a, ((0, 0), (0, tq_a - t), (0, 0)))
        o_a = _mixer_a(padq(qa), padq(qi), padq(wi), ki_all, k_all, v_all, past=past, valid_len=l_valid,
                       topk=min(TOPK_MAX, l_valid // 4), tq=tq_a, tk=tk_a)[:, :t]
        rows = cbk.shape[1]
        assert past % CHUNK == 0 and rows % CHUNK == 0
        lb_valid = rows + t
        lb = -(-lb_valid // LANES) * LANES
        kv_cache = jnp.concatenate([cbk.reshape(b, rows, W_B), cbv.reshape(b, rows, W_B)], axis=2).astype(BF16)
        kv_all = jnp.concatenate([kv_cache, kvb16, jnp.zeros((b, lb - lb_valid, 2 * W_B), BF16)], axis=1)
        bias = _band_bias(rel_bias, tq_a, lb, rows, [0], lb_valid)
        o_b = _mixer_b(padq(qb), kv_all, bias, tq=tq_a, tkb=lb, nkb=1, back=0)[:, :t]

    (tail32,) = _mm(h[:, t - rows_tail:].reshape(b * rows_tail, d), wts["kvb"], (F32,))
    tail32 = tail32.reshape(b, rows_tail, 2, N_HEADS_B, HEAD_DIM)
    new_bk, new_bv = tail32[:, :, 0], tail32[:, :, 1]

    merged = _merge(h2d, o_a.reshape(b * t, W_A), o_b.reshape(b * t, W_B), wts["pa"], wts["pb"],
                    wts["gate"], b_gate)
    x1, h2 = _outproj(merged.reshape(b, t, d), x, wts["out"], mod, norm2_g)
    y, tails = _ffn(h2, x1, mod, wts["up_a"], wts["up_b"], wts["down"], w_conv, b_conv,
                    conv_prev, final_g)
    conv_state = tails[:, -1]
    state = (ka32.reshape(b, t, N_HEADS_A, HEAD_DIM), va32.reshape(b, t, N_HEADS_A, HEAD_DIM), ki,
             new_bk, new_bv, conv_state)
    return y, state


def kernel(x_prompt, x_sample, cache_a_k, cache_a_v, cache_idx_k, cache_b_k, cache_b_v, state_ffn_conv,
           c_prompt, c_sample, w_ada, b_ada, norm1_g, w_in, rel_bias, w_gate, b_gate, w_proj_a, w_proj_b,
           w_out, norm2_g, w_up, w_conv, b_conv, w_down, final_g):
    depth = w_ada.shape[0]
    assert depth == 1, "the fused final RMSNorm assumes a single layer"
    d = x_prompt.shape[-1]
    nb_p = x_prompt.shape[0]
    c_all = jnp.concatenate([c_prompt, c_sample], axis=0)
    xp, xs = x_prompt, x_sample
    states_p, states_s = [], []
    for l in range(depth):
        wts = _prep_weights(w_in[l], w_gate[l], w_proj_a[l], w_proj_b[l], w_out[l], w_up[l], w_down[l])
        mod = _ada(c_all, w_ada[l], b_ada[l]).reshape(c_all.shape[0], 6, d)
        args = (wts, norm1_g[l], rel_bias[l], b_gate[l], norm2_g[l], w_conv[l], b_conv[l], final_g)
        xp, st_p = _trunk_layer(xp, mod[:nb_p], None, *args)
        states_p.append(st_p)
        cache_l = (cache_a_k[l], cache_a_v[l], cache_idx_k[l], cache_b_k[l], cache_b_v[l], state_ffn_conv[l])
        xs, st_s = _trunk_layer(xs, mod[nb_p:], cache_l, *args)
        states_s.append(st_s)
    sp = [t[0][None] for t in zip(*states_p)]
    ss = [t[0][None] for t in zip(*states_s)]
    return (xp, xs, *sp, *ss)
```

```python
import functools

import jax
import jax.numpy as jnp
import numpy as np
from jax import lax
from jax.experimental import pallas as pl
from jax.experimental.pallas import tpu as pltpu

F32 = jnp.float32
BF16 = jnp.bfloat16

CHUNK = 64
CHUNK_SHIFT = 6
HEAD_DIM = 128
N_HEADS_A = 8
N_IDX_HEADS = 16
IDX_DIM = 64
TOPK_MAX = 256
N_HEADS_B = 8
BAND_CHUNKS = 8
BAND_PAST = BAND_CHUNKS * CHUNK
REL_CLIP = 2 * CHUNK
CONV_W = 3
EPS = 1e-6
NEG = -1e30
W_A = N_HEADS_A * HEAD_DIM
W_B = N_HEADS_B * HEAD_DIM
ATTN_SCALE = HEAD_DIM ** -0.5
LOG2E = 1.4426950408889634
POS_UNBOUNDED = 2 ** 30
V_ROWS = HEAD_DIM + 16
HALF_BITS = 16
HALF_MASK = 0xFFFF
HALF_BIAS = 1 << (HALF_BITS - 1)
LANES = 128
MXU_COLS = 256
VMEM_LIMIT = 56 * 1024 * 1024
FFN_ACC_BYTES = 8 * 1024 * 1024


def _cp(*sem):
    return pltpu.CompilerParams(dimension_semantics=sem, vmem_limit_bytes=VMEM_LIMIT)


def _pick(dim, pref, align):
    t = min(pref, dim)
    t -= t % align
    while t >= align:
        if dim % t == 0:
            return t
        t -= align
    return dim


def _ada_kernel(c_ref, w_ref, b_ref, o_ref):
    c = c_ref[...]
    s = c * jax.nn.sigmoid(c)
    o_ref[...] = jnp.dot(s, w_ref[...], preferred_element_type=F32,
                         precision=lax.Precision.HIGHEST) + b_ref[...]


def _ada(c, w, b):
    r, d = c.shape
    n = w.shape[1]
    tn = _pick(n, 1024, LANES)
    return pl.pallas_call(
        _ada_kernel,
        grid=(n // tn,),
        in_specs=[pl.BlockSpec((r, d), lambda j: (0, 0)),
                  pl.BlockSpec((d, tn), lambda j: (0, j)),
                  pl.BlockSpec((1, tn), lambda j: (0, j))],
        out_specs=pl.BlockSpec((r, tn), lambda j: (0, j)),
        out_shape=jax.ShapeDtypeStruct((r, n), F32),
        compiler_params=_cp("arbitrary"),
        name="ada",
    )(c, w, b.reshape(1, n))


def _rms_mod(x, g, shift, scale):
    y = x * lax.rsqrt(jnp.mean(x * x, axis=-1, keepdims=True) + EPS) * g
    return y * (1.0 + scale) + shift


def _split_bf16(x):
    hi = x.astype(BF16)
    return hi, (x - hi.astype(F32)).astype(BF16)


def _norm_mod_kernel(x_ref, g_ref, mod_ref, hi_ref, lo_ref):
    h = _rms_mod(x_ref[0], g_ref[...], mod_ref[0, 0:1, :], mod_ref[0, 1:2, :])
    hi, lo = _split_bf16(h)
    hi_ref[0] = hi
    lo_ref[0] = lo


def _norm_mod(x, g, mod):
    b, t, d = x.shape
    tm = _pick(t, 512, 16)
    row = lambda i, j: (i, j, 0)
    return pl.pallas_call(
        _norm_mod_kernel,
        grid=(b, t // tm),
        in_specs=[pl.BlockSpec((1, tm, d), row),
                  pl.BlockSpec((1, d), lambda i, j: (0, 0)),
                  pl.BlockSpec((1, 6, d), lambda i, j: (i, 0, 0))],
        out_specs=[pl.BlockSpec((1, tm, d), row), pl.BlockSpec((1, tm, d), row)],
        out_shape=[jax.ShapeDtypeStruct((b, t, d), BF16), jax.ShapeDtypeStruct((b, t, d), BF16)],
        compiler_params=_cp("arbitrary", "arbitrary"),
        name="norm_mod",
    )(x, g.reshape(1, d), mod)


def _mm_kernel(a_ref, w_ref, *o_refs, scale):
    acc = jnp.dot(a_ref[...], w_ref[...], preferred_element_type=F32)
    if scale is not None:
        acc = acc * scale
    for o_ref in o_refs:
        o_ref[...] = acc.astype(o_ref.dtype).reshape(o_ref.shape)


def _mm(a, w, dtypes, scale=None, heads_out=()):
    m, k = a.shape
    n = w.shape[1]
    tm = _pick(m, 1024, 16)
    tn = _pick(n, 1024, LANES)
    nh = tn // HEAD_DIM
    specs = [pl.BlockSpec((tm, nh, HEAD_DIM), lambda i, j: (i, j, 0)) if o in heads_out
             else pl.BlockSpec((tm, tn), lambda i, j: (i, j)) for o in range(len(dtypes))]
    shapes = [jax.ShapeDtypeStruct((m, n // HEAD_DIM, HEAD_DIM) if o in heads_out else (m, n), dt)
              for o, dt in enumerate(dtypes)]
    outs = pl.pallas_call(
        functools.partial(_mm_kernel, scale=scale),
        grid=(m // tm, n // tn),
        in_specs=[pl.BlockSpec((tm, k), lambda i, j: (i, 0)),
                  pl.BlockSpec((k, tn), lambda i, j: (0, j))],
        out_specs=specs,
        out_shape=shapes,
        compiler_params=_cp("arbitrary", "arbitrary"),
        name="proj",
    )(a, w)
    return outs


def _mm_split_kernel(ah_ref, al_ref, wh_ref, wl_ref, o_ref):
    ah = ah_ref[...]
    acc = jnp.dot(ah, wh_ref[...], preferred_element_type=F32)
    acc = acc + jnp.dot(al_ref[...], wh_ref[...], preferred_element_type=F32)
    acc = acc + jnp.dot(ah, wl_ref[...], preferred_element_type=F32)
    o_ref[...] = acc


def _mm_split(a_hi, a_lo, w_hi, w_lo):
    m, k = a_hi.shape
    n = w_hi.shape[1]
    tm = _pick(m, 1024, 16)
    tn = _pick(n, 1024, LANES)
    a_spec = pl.BlockSpec((tm, k), lambda i, j: (i, 0))
    w_spec = pl.BlockSpec((k, tn), lambda i, j: (0, j))
    return pl.pallas_call(
        _mm_split_kernel,
        grid=(m // tm, n // tn),
        in_specs=[a_spec, a_spec, w_spec, w_spec],
        out_specs=pl.BlockSpec((tm, tn), lambda i, j: (i, j)),
        out_shape=jax.ShapeDtypeStruct((m, n), F32),
        compiler_params=_cp("arbitrary", "arbitrary"),
        name="proj_split",
    )(a_hi, a_lo, w_hi, w_lo)


def _mixer_a_kernel(tile_ref, block_ref, qa_ref, qi_ref, wit_ref, ki2_ref, k_ref, vt_ref, o_ref,
                    key_sc, hi_sc, lo_sc, thr_sc, cut_sc, tie_sc, dist_sc, m_sc, acc_sc, qx_sc,
                    *, tq, tk, past, valid_len, topk):
    i = tile_ref[pl.program_id(1)]
    j = block_ref[pl.program_id(1)]
    q0 = past + i * tq
    kend = jnp.minimum(((q0 + tq + CHUNK - 1) // CHUNK) * CHUNK, valid_len)
    nblk = (kend + tk - 1) // tk
    slab16 = 16

    def admissible(jb, row0=0, rows=tk):
        kpos = jb * tk + row0 + lax.broadcasted_iota(jnp.int32, (rows, tq), 0)
        qpos = q0 + lax.broadcasted_iota(jnp.int32, (rows, tq), 1)
        adm = (jnp.right_shift(kpos, CHUNK_SHIFT) <= jnp.right_shift(qpos, CHUNK_SHIFT)) & (kpos < valid_len)
        return adm, qpos, kpos

    @pl.when(j == 0)
    def _index_and_threshold():
        qt = qi_ref[0].T
        q_hi, q_lo = _split_bf16(qt)
        for h in range(N_IDX_HEADS):
            rows = slice(h * IDX_DIM, (h + 1) * IDX_DIM)
            qx_sc[h, 0 * IDX_DIM:1 * IDX_DIM, :] = q_hi[rows]
            qx_sc[h, 1 * IDX_DIM:2 * IDX_DIM, :] = q_hi[rows]
            qx_sc[h, 2 * IDX_DIM:3 * IDX_DIM, :] = q_lo[rows]
            qx_sc[h, 3 * IDX_DIM:4 * IDX_DIM, :] = q_lo[rows]


        nsplit = 2 if tk % (2 * MXU_COLS) == 0 else 1
        ts = tk // nsplit
        first = lax.broadcasted_iota(jnp.int32, (ts, LANES), 1) < IDX_DIM

        def score_block(jb, carry):
            for r in range(nsplit):
                rows = slice(r * ts, (r + 1) * ts)
                dup = ki2_ref[0, jb, rows, :]
                kk = jnp.where(first, dup, dup - dup.astype(BF16).astype(F32)).astype(BF16)
                kx = jnp.concatenate([kk, kk], axis=1)
                acc = jnp.zeros((ts, tq), F32)
                for h in range(N_IDX_HEADS):
                    rel = jnp.dot(kx, qx_sc[h], preferred_element_type=F32)
                    acc = acc + wit_ref[0, h:h + 1, :] * jnp.maximum(rel, 0.0)
                acc = jnp.where(admissible(jb, r * ts, ts)[0], acc, NEG)
                bits = pltpu.bitcast(acc, jnp.int32)
                key = bits ^ (jnp.right_shift(bits, 31) & 0x7FFFFFFF)
                key_sc[jb, rows, :] = key
                hi_sc[jb, rows, :] = jnp.right_shift(key, HALF_BITS).astype(jnp.int16)
                lo_sc[jb, rows, :] = ((key & HALF_MASK) - HALF_BIAS).astype(jnp.int16)
            return carry

        lax.fori_loop(0, nblk, score_block, 0)

        def count_ge(src_sc, cand):
            cand16 = cand.astype(jnp.int16)

            def count_block(jb, cnt):
                part = [None] * 4
                for r in range(tk // slab16):
                    rows = slice(r * slab16, (r + 1) * slab16)
                    hit = jnp.where(src_sc[jb, rows, :] >= cand16, jnp.int16(1), jnp.int16(0))
                    part[r % 4] = hit if part[r % 4] is None else part[r % 4] + hit
                return cnt + ((part[0] + part[1]) + (part[2] + part[3]))

            cnt = lax.fori_loop(0, nblk, count_block, jnp.zeros((slab16, tq), jnp.int16))
            return jnp.sum(cnt.astype(jnp.int32), axis=0, keepdims=True)

        def kth_largest16(src_sc, want, n_all):
            def bit_step(b, state):
                t_u, n_t = state
                cand_u = t_u | lax.shift_left(jnp.int32(1), HALF_BITS - 1 - b)
                n_cand = count_ge(src_sc, cand_u - HALF_BIAS)
                take = n_cand >= want
                return jnp.where(take, cand_u, t_u), jnp.where(take, n_cand, n_t)
            t_u, n_t = lax.fori_loop(0, HALF_BITS, bit_step, (jnp.zeros((slab16, tq), jnp.int32), n_all))
            return t_u - HALF_BIAS, n_t

        t_hi, n_hi = kth_largest16(hi_sc, topk, jnp.broadcast_to(nblk * tk, (1, tq)))
        above = jnp.where(t_hi == HALF_BIAS - 1, 0, count_ge(hi_sc, jnp.minimum(t_hi + 1, HALF_BIAS - 1)))
        t_hi16 = t_hi.astype(jnp.int16)

        def keep_group(jb, carry):
            for r in range(tk // slab16):
                rows = slice(r * slab16, (r + 1) * slab16)
                lo_sc[jb, rows, :] = jnp.where(hi_sc[jb, rows, :] == t_hi16, lo_sc[jb, rows, :],
                                               jnp.int16(-HALF_BIAS))
            return carry

        lax.fori_loop(0, nblk, keep_group, 0)
        above = above[0:1, :]
        t_lo, n_lo = kth_largest16(lo_sc, topk - above, n_hi - above)
        thr = (t_hi * (2 * HALF_BIAS) + (t_lo + HALF_BIAS))[0:1, :]
        thr_sc[...] = jnp.broadcast_to(thr, thr_sc.shape)

        def count_keys(pred):
            def count_block(jb, cnt):
                hit = pred(key_sc[jb], admissible(jb)[2])
                return cnt + jnp.sum(jnp.where(hit, 1, 0), axis=0, keepdims=True)
            return lax.fori_loop(0, nblk, count_block, jnp.zeros((1, tq), jnp.int32))

        surplus = jnp.max(above + n_lo) - topk
        tie_sc[0] = surplus
        cut_sc[...] = jnp.full(cut_sc.shape, POS_UNBOUNDED, jnp.int32)

        @pl.when(surplus > 0)
        def _break_ties():
            keep = topk - count_keys(lambda key, kpos: key > thr)
            pos_bits = (key_sc.shape[0] * tk).bit_length()

            def bit_step(b, x):
                cand = x | lax.shift_left(jnp.int32(1), pos_bits - 1 - b)
                below = count_keys(lambda key, kpos: (key == thr) & (kpos < cand))
                return jnp.where(below <= keep, cand, x)

            x = lax.fori_loop(0, pos_bits, bit_step, jnp.zeros((1, tq), jnp.int32))
            cut_sc[...] = jnp.broadcast_to(x, cut_sc.shape)

        m_sc[...] = jnp.full(m_sc.shape, -jnp.inf, F32)
        acc_sc[...] = jnp.zeros(acc_sc.shape, F32)

    def _attend():
        def masked_distance(select):
            adm, qpos, kpos = admissible(j)
            key = key_sc[j]
            dist_sc[...] = jnp.where(select(key, kpos) & adm, jnp.abs(qpos - kpos).astype(F32), -NEG)

        @pl.when(tie_sc[0] <= 0)
        def _():
            masked_distance(lambda key, kpos: key >= thr_sc[0:1, :])

        @pl.when(tie_sc[0] > 0)
        def _():
            masked_distance(lambda key, kpos: (key > thr_sc[0:1, :])
                            | ((key == thr_sc[0:1, :]) & (kpos < cut_sc[0:1, :])))

        nch = 2 if tk % (2 * MXU_COLS) == 0 else 1
        ck = tk // nch

        def logits(h, c):
            hs = slice(h * HEAD_DIM, (h + 1) * HEAD_DIM)
            return lax.dot_general(k_ref[0, c * ck:(c + 1) * ck, hs], qa_ref[0, :, hs],
                                   (((1,), (1,)), ((), ())), preferred_element_type=F32)

        s_next = [logits(0, c) for c in range(nch)]
        for h in range(N_HEADS_A):
            hx = slice(h * V_ROWS, (h + 1) * V_ROWS)
            s_raw, s_next = s_next, []
            m_prev = m_sc[h:h + 1, :]
            m_new = m_prev
            s = []
            for c in range(nch):
                if h + 1 < N_HEADS_A:
                    s_next.append(logits(h + 1, c))
                s.append(s_raw[c] - ((2.0 ** -(h + 1)) * LOG2E) * dist_sc[c * ck:(c + 1) * ck, :])
                m_new = jnp.maximum(m_new, jnp.max(s[c], axis=0, keepdims=True))
            acc = jnp.exp2(m_prev - m_new) * acc_sc[hx, :]
            for c in range(nch):
                p = jnp.exp2(s[c] - m_new)
                acc = acc + jnp.dot(vt_ref[0, hx, c * ck:(c + 1) * ck], p.astype(BF16),
                                    preferred_element_type=F32)
            acc_sc[hx, :] = acc
            m_sc[h:h + 1, :] = m_new

    _attend()

    @pl.when(j == nblk - 1)
    def _finalize():
        for h in range(N_HEADS_A):
            hs = slice(h * HEAD_DIM, (h + 1) * HEAD_DIM)
            num = acc_sc[h * V_ROWS:h * V_ROWS + HEAD_DIM, :]
            den = acc_sc[h * V_ROWS + HEAD_DIM:h * V_ROWS + HEAD_DIM + 1, :]
            o_ref[0, :, hs] = (num / den).T.astype(o_ref.dtype)


def _mixer_a(q2, qi, wi, ki2, k_all, v_all, *, past, valid_len, topk, tq, tk):
    b, t, _ = q2.shape
    lp = k_all.shape[1]
    nkb = lp // tk
    assert lp % tk == 0 and t % tq == 0 and tk % LANES == 0 and past % CHUNK == 0
    ki2 = ki2.reshape(b, nkb, tk, ki2.shape[2])
    wit = jnp.swapaxes(wi, 1, 2)
    vt = jnp.swapaxes(v_all, 1, 2).reshape(b, N_HEADS_A, HEAD_DIM, lp)
    vt = jnp.concatenate([vt, jnp.ones((b, N_HEADS_A, V_ROWS - HEAD_DIM, lp), BF16)], axis=2)
    vt = vt.reshape(b, N_HEADS_A * V_ROWS, lp)

    def blocks_needed(i):
        kend = min(-(-(past + (i + 1) * tq) // CHUNK) * CHUNK, valid_len)
        return -(-kend // tk)

    pairs = [(i, j) for i in range(t // tq) for j in range(blocks_needed(i))]
    tile_of = jnp.asarray([p[0] for p in pairs], jnp.int32)
    block_of = jnp.asarray([p[1] for p in pairs], jnp.int32)
    q_map = lambda bb, p, ti, bj: (bb, ti[p], 0)
    kernel = functools.partial(_mixer_a_kernel, tq=tq, tk=tk, past=past, valid_len=valid_len, topk=topk)
    grid_spec = pltpu.PrefetchScalarGridSpec(
        num_scalar_prefetch=2,
        grid=(b, len(pairs)),
        in_specs=[pl.BlockSpec((1, tq, W_A), q_map),
                  pl.BlockSpec((1, tq, N_IDX_HEADS * IDX_DIM), q_map),
                  pl.BlockSpec((1, N_IDX_HEADS, tq), lambda bb, p, ti, bj: (bb, 0, ti[p])),
                  pl.BlockSpec((1, nkb, tk, 2 * IDX_DIM), lambda bb, p, ti, bj: (bb, 0, 0, 0)),
                  pl.BlockSpec((1, tk, W_A), lambda bb, p, ti, bj: (bb, bj[p], 0)),
                  pl.BlockSpec((1, N_HEADS_A * V_ROWS, tk), lambda bb, p, ti, bj: (bb, 0, bj[p]))],
        out_specs=pl.BlockSpec((1, tq, W_A), q_map),
        scratch_shapes=[pltpu.VMEM((nkb, tk, tq), jnp.int32),
                        pltpu.VMEM((nkb, tk, tq), jnp.int16),
                        pltpu.VMEM((nkb, tk, tq), jnp.int16),
                        pltpu.VMEM((8, tq), jnp.int32),
                        pltpu.VMEM((8, tq), jnp.int32),
                        pltpu.SMEM((1,), jnp.int32),
                        pltpu.VMEM((tk, tq), F32),
                        pltpu.VMEM((N_HEADS_A, tq), F32),
                        pltpu.VMEM((N_HEADS_A * V_ROWS, tq), F32),
                        pltpu.VMEM((N_IDX_HEADS, 4 * IDX_DIM, tq), BF16)])
    return pl.pallas_call(
        kernel,
        grid_spec=grid_spec,
        out_shape=jax.ShapeDtypeStruct((b, t, W_A), BF16),
        compiler_params=_cp("arbitrary", "arbitrary"),
        name="mixer_a",
    )(tile_of, block_of, q2, qi, wit, ki2, k_all, vt)


def _mixer_b_kernel(*refs, nkb, tkb):
    q_ref = refs[0]
    k_refs = refs[1:1 + nkb]
    v_refs = refs[1 + nkb:1 + 2 * nkb]
    bias_ref = refs[1 + 2 * nkb]
    o_ref = refs[2 + 2 * nkb]
    def logits(h, c):
        hs = slice(h * HEAD_DIM, (h + 1) * HEAD_DIM)
        return lax.dot_general(k_refs[c][0, :, hs], q_ref[0, :, hs], (((1,), (1,)), ((), ())),
                               preferred_element_type=F32)

    s_next = [logits(0, c) for c in range(nkb)]
    for h in range(N_HEADS_B):
        hs = slice(h * HEAD_DIM, (h + 1) * HEAD_DIM)
        s_raw, s_next = s_next, []
        s = []
        m = None
        for c in range(nkb):
            if h + 1 < N_HEADS_B:
                s_next.append(logits(h + 1, c))
            s.append(s_raw[c] * (ATTN_SCALE * LOG2E) + bias_ref[0, h, c * tkb:(c + 1) * tkb, :])
            mc = jnp.max(s[c], axis=0, keepdims=True)
            m = mc if m is None else jnp.maximum(m, mc)
        l = None
        pv = None
        for c, vr in enumerate(v_refs):
            p = jnp.exp2(s[c] - m)
            lc = jnp.sum(p, axis=0, keepdims=True)
            t = lax.dot_general(vr[0, :, hs], p.astype(BF16), (((0,), (0,)), ((), ())),
                                preferred_element_type=F32)
            l = lc if l is None else l + lc
            pv = t if pv is None else pv + t
        o_ref[0, :, hs] = (pv / l).T.astype(o_ref.dtype)


def _band_bias(rel_bias, tq, wk, off, lows, hi):
    nh = rel_bias.shape[0]
    n = wk + tq
    e = np.concatenate([np.arange(wk + 1), np.arange(-(tq - 1), 0)])
    f = rel_bias[:, np.clip(off - e, -REL_CLIP, REL_CLIP) + REL_CLIP].astype(F32) * LOG2E
    bias = jnp.tile(f, (1, tq))[:, :tq * (n - 1)].reshape(nh, tq, n - 1)[:, :, :wk]
    bias = jnp.swapaxes(bias, 1, 2)
    c = np.arange(wk)[:, None]
    r = np.arange(tq)[None, :]
    dq = r // CHUNK
    dk = np.floor_divide(c - off, CHUNK)
    ok = (dk <= dq) & (dk >= dq - BAND_CHUNKS) & (c < hi)
    tiles = [jnp.where((ok & (c >= lo))[None], bias, NEG) for lo in lows]
    return jnp.stack(tiles)


def _mixer_b(q2, kv, bias, *, tq, tkb, nkb, back):
    b, t, _ = q2.shape
    nvar = bias.shape[0]
    wk = nkb * tkb
    kmaps = [functools.partial(lambda bb, i, p, col: (bb, jnp.maximum(i - back + p, 0), col), p=p, col=0)
             for p in range(nkb)]
    vmaps = [functools.partial(lambda bb, i, p, col: (bb, jnp.maximum(i - back + p, 0), col), p=p, col=1)
             for p in range(nkb)]
    kernel = functools.partial(_mixer_b_kernel, nkb=nkb, tkb=tkb)
    return pl.pallas_call(
        kernel,
        grid=(b, t // tq),
        in_specs=([pl.BlockSpec((1, tq, W_B), lambda bb, i: (bb, i, 0))]
                  + [pl.BlockSpec((1, tkb, W_B), m) for m in kmaps]
                  + [pl.BlockSpec((1, tkb, W_B), m) for m in vmaps]
                  + [pl.BlockSpec((1, N_HEADS_B, wk, tq), lambda bb, i: (jnp.minimum(i, nvar - 1), 0, 0, 0))]),
        out_specs=pl.BlockSpec((1, tq, W_B), lambda bb, i: (bb, i, 0)),
        out_shape=jax.ShapeDtypeStruct((b, t, W_B), BF16),
        compiler_params=_cp("arbitrary", "arbitrary"),
        name="mixer_b",
    )(q2, *([kv] * (2 * nkb)), bias)


def _merge_kernel(h_ref, oa_ref, ob_ref, pa_ref, pb_ref, wga_ref, wgb_ref, bga_ref, bgb_ref, o_ref):
    h = h_ref[...]
    ga = jax.nn.sigmoid(jnp.dot(h, wga_ref[...], preferred_element_type=F32) + bga_ref[...])
    gb = jax.nn.sigmoid(jnp.dot(h, wgb_ref[...], preferred_element_type=F32) + bgb_ref[...])
    ya = jnp.dot(oa_ref[...], pa_ref[...], preferred_element_type=F32)
    yb = jnp.dot(ob_ref[...], pb_ref[...], preferred_element_type=F32)
    o_ref[...] = (ga * ya + gb * yb).astype(o_ref.dtype)


def _merge(h, oa, ob, pa, pb, wg, bg):
    m, d = h.shape
    tm = _pick(m, 512, 16)
    tn = _pick(d, 512, LANES)
    nb = d // tn
    bg2 = bg.reshape(1, 2 * d)
    return pl.pallas_call(
        _merge_kernel,
        grid=(m // tm, nb),
        in_specs=[pl.BlockSpec((tm, d), lambda i, j: (i, 0)),
                  pl.BlockSpec((tm, W_A), lambda i, j: (i, 0)),
                  pl.BlockSpec((tm, W_B), lambda i, j: (i, 0)),
                  pl.BlockSpec((W_A, tn), lambda i, j: (0, j)),
                  pl.BlockSpec((W_B, tn), lambda i, j: (0, j)),
                  pl.BlockSpec((d, tn), lambda i, j: (0, j)),
                  pl.BlockSpec((d, tn), lambda i, j: (0, j + nb)),
                  pl.BlockSpec((1, tn), lambda i, j: (0, j)),
                  pl.BlockSpec((1, tn), lambda i, j: (0, j + nb))],
        out_specs=pl.BlockSpec((tm, tn), lambda i, j: (i, j)),
        out_shape=jax.ShapeDtypeStruct((m, d), BF16),
        compiler_params=_cp("arbitrary", "arbitrary"),
        name="merge",
    )(h, oa, ob, pa, pb, wg, wg, bg2, bg2)


def _outproj_kernel(mg_ref, x_ref, w_ref, mod_ref, g_ref, x1_ref, h2_ref):
    o = jnp.dot(mg_ref[0], w_ref[...], preferred_element_type=F32)
    x1 = x_ref[0] + mod_ref[0, 2:3, :] * o
    x1_ref[0] = x1
    h2_ref[0] = _rms_mod(x1, g_ref[...], mod_ref[0, 3:4, :], mod_ref[0, 4:5, :]).astype(h2_ref.dtype)


def _outproj(merged, x, w_out, mod, g2):
    b, t, d = x.shape
    tm = _pick(t, 256, 16)
    row = lambda i, j: (i, j, 0)
    return pl.pallas_call(
        _outproj_kernel,
        grid=(b, t // tm),
        in_specs=[pl.BlockSpec((1, tm, d), row),
                  pl.BlockSpec((1, tm, d), row),
                  pl.BlockSpec((d, d), lambda i, j: (0, 0)),
                  pl.BlockSpec((1, 6, d), lambda i, j: (i, 0, 0)),
                  pl.BlockSpec((1, d), lambda i, j: (0, 0))],
        out_specs=[pl.BlockSpec((1, tm, d), row), pl.BlockSpec((1, tm, d), row)],
        out_shape=[jax.ShapeDtypeStruct((b, t, d), F32), jax.ShapeDtypeStruct((b, t, d), BF16)],
        compiler_params=_cp("arbitrary", "arbitrary"),
        name="outproj",
    )(merged, x, w_out, mod, g2.reshape(1, d))


def _ffn_kernel(h2_ref, x1_ref, mod_ref, wa_ref, wb_ref, wd_ref, wc_ref, bc_ref, prev_ref, fg_ref,
                y_ref, st_ref, abuf, carry, acc_sc, *, tm, nf, weights_outer):
    if weights_outer:
        f, slot, m = pl.program_id(0), pl.program_id(1), pl.program_id(2)
    else:
        slot, m, f = 0, pl.program_id(1), pl.program_id(2)

    @pl.when(f == 0)
    def _():
        acc_sc[slot] = jnp.zeros(acc_sc.shape[1:], F32)

    h2 = h2_ref[0]
    prev = jnp.where(m == 0, prev_ref[0], carry[f])
    tf = wa_ref.shape[1]
    nsub = 2 if tf % (2 * MXU_COLS) == 0 else 1
    tfs = tf // nsub

    def up(c):
        cols = slice(c * tfs, (c + 1) * tfs)
        return (jnp.dot(h2, wa_ref[:, cols], preferred_element_type=F32),
                jnp.dot(h2, wb_ref[:, cols], preferred_element_type=F32))

    nxt = up(0)
    for c in range(nsub):
        cols = slice(c * tfs, (c + 1) * tfs)
        a, gate = nxt
        if c + 1 < nsub:
            nxt = up(c + 1)
        abuf[6:8, cols] = prev[:, cols]
        abuf[8:8 + tm, cols] = a
        tail = a[tm - 2:tm, :]
        carry[f, :, cols] = tail
        st_ref[0, 0, :, cols] = tail
        conv = (bc_ref[:, cols] + wc_ref[0:1, cols] * abuf[6:6 + tm, cols]
                + wc_ref[1:2, cols] * abuf[7:7 + tm, cols] + wc_ref[2:3, cols] * a)
        u = (conv * jax.nn.sigmoid(conv) * gate).astype(BF16)
        acc_sc[slot] += jnp.dot(u, wd_ref[cols, :], preferred_element_type=F32)

    @pl.when(f == nf - 1)
    def _():
        x2 = x1_ref[0] + mod_ref[0, 5:6, :] * acc_sc[slot]
        y_ref[0] = x2 * lax.rsqrt(jnp.mean(x2 * x2, axis=-1, keepdims=True) + EPS) * fg_ref[...]


def _ffn(h2, x1, mod, wa, wb, wd, wconv, bconv, prev, final_g):
    b, t, d = x1.shape
    dff = wa.shape[1]
    tm = _pick(t, 512, 16)
    tf = _pick(dff, 512, LANES)
    nf = dff // tf
    nm = t // tm
    weights_outer = nm == 1 and b * tm * d * 4 <= FFN_ACC_BYTES
    if weights_outer:
        grid = (nf, b, nm)
        spec = lambda shape, fn: pl.BlockSpec(shape, lambda k, i, j: fn(i, j, k))
        row = lambda i, j, k: (jnp.where(k == nf - 1, i, 0), jnp.where(k == nf - 1, j, 0), 0)
    else:
        grid = (b, nm, nf)
        spec = lambda shape, fn: pl.BlockSpec(shape, fn)
        row = lambda i, j, k: (i, j, 0)
    kernel = functools.partial(_ffn_kernel, tm=tm, nf=nf, weights_outer=weights_outer)
    return pl.pallas_call(
        kernel,
        grid=grid,
        in_specs=[spec((1, tm, d), lambda i, j, k: (i, j, 0)),
                  spec((1, tm, d), row),
                  spec((1, 6, d), lambda i, j, k: (i, 0, 0)),
                  spec((d, tf), lambda i, j, k: (0, k)),
                  spec((d, tf), lambda i, j, k: (0, k)),
                  spec((tf, d), lambda i, j, k: (k, 0)),
                  spec((CONV_W, tf), lambda i, j, k: (0, k)),
                  spec((1, tf), lambda i, j, k: (0, k)),
                  spec((1, CONV_W - 1, tf), lambda i, j, k: (i, 0, k)),
                  spec((1, d), lambda i, j, k: (0, 0))],
        out_specs=[spec((1, tm, d), row),
                   spec((1, 1, CONV_W - 1, tf), lambda i, j, k: (i, j, 0, k))],
        out_shape=[jax.ShapeDtypeStruct((b, t, d), F32),
                   jax.ShapeDtypeStruct((b, nm, CONV_W - 1, dff), F32)],
        scratch_shapes=[pltpu.VMEM((tm + 8, tf), F32),
                        pltpu.VMEM((nf, CONV_W - 1, tf), F32),
                        pltpu.VMEM((b if weights_outer else 1, tm, d), F32)],
        compiler_params=_cp("arbitrary", "arbitrary", "arbitrary"),
        name="ffn",
    )(h2, x1, mod, wa, wb, wd, wconv, bconv.reshape(1, dff), prev, final_g.reshape(1, d))


def _prep_weights(w_in, w_gate, w_proj_a, w_proj_b, w_out, w_up, w_down):
    o = np.cumsum((0, W_A, W_A, W_A, N_IDX_HEADS * IDX_DIM, IDX_DIM, N_IDX_HEADS, W_B, W_B, W_B))
    col = lambda a, b: w_in[:, int(o[a]):int(o[b])]
    d = w_in.shape[0]
    dff = w_down.shape[0]
    pad = jnp.zeros((d, MXU_COLS - 2 * IDX_DIM - N_IDX_HEADS), w_in.dtype)
    qi_hi, qi_lo = _split_bf16(col(3, 4))
    kiwi_hi, kiwi_lo = _split_bf16(jnp.concatenate([col(4, 5), col(4, 5), col(5, 6), pad], axis=1))
    return dict(
        qa=col(0, 1).astype(BF16),
        qb=col(6, 7).astype(BF16),
        ka=col(1, 2).astype(BF16),
        va=col(2, 3).astype(BF16),
        qi_hi=qi_hi, qi_lo=qi_lo, kiwi_hi=kiwi_hi, kiwi_lo=kiwi_lo,
        kvb=col(7, 9).astype(BF16),
        gate=w_gate.astype(BF16),
        pa=w_proj_a.astype(BF16),
        pb=w_proj_b.astype(BF16),
        out=w_out.astype(BF16),
        up_a=w_up[:, :dff].astype(BF16),
        up_b=w_up[:, dff:].astype(BF16),
        down=w_down.astype(BF16),
    )


def _trunk_layer(x, mod, cache, wts, norm1_g, rel_bias, b_gate, norm2_g, w_conv, b_conv, final_g):
    b, t, d = x.shape
    dff = wts["down"].shape[0]
    h, h_lo = _norm_mod(x, norm1_g, mod)
    h2d = h.reshape(b * t, d)
    hl2d = h_lo.reshape(b * t, d)
    (qa,) = _mm(h2d, wts["qa"], (BF16,), scale=ATTN_SCALE * LOG2E)
    (qb,) = _mm(h2d, wts["qb"], (BF16,))
    ka32, ka16 = _mm(h2d, wts["ka"], (F32, BF16), heads_out=(0,))
    va32, va16 = _mm(h2d, wts["va"], (F32, BF16), heads_out=(0,))
    qi = _mm_split(h2d, hl2d, wts["qi_hi"], wts["qi_lo"]).reshape(b, t, N_IDX_HEADS * IDX_DIM)
    kiwi = _mm_split(h2d, hl2d, wts["kiwi_hi"], wts["kiwi_lo"])
    (kvb16,) = _mm(h2d, wts["kvb"], (BF16,))
    qa = qa.reshape(b, t, W_A)
    qb = qb.reshape(b, t, W_B)
    kiwi = kiwi.reshape(b, t, MXU_COLS)
    ki = kiwi[:, :, :IDX_DIM]
    wi = kiwi[:, :, 2 * IDX_DIM:2 * IDX_DIM + N_IDX_HEADS]
    ka16 = ka16.reshape(b, t, W_A)
    va16 = va16.reshape(b, t, W_A)
    kvb16 = kvb16.reshape(b, t, 2 * W_B)

    if cache is None:
        rows_tail = min(BAND_PAST, t)
        tq_a = _pick(t, 256, CHUNK)
        tk_a = _pick(t, 512, LANES)
        o_a = _mixer_a(qa, qi, wi, kiwi, ka16, va16, past=0, valid_len=t, topk=min(TOPK_MAX, t // 4),
                       tq=tq_a, tk=tk_a)
        tq_b = _pick(t, 256, CHUNK)
        assert BAND_PAST % tq_b == 0
        back = BAND_PAST // tq_b
        lows = [BAND_PAST - v * tq_b for v in range(back + 1)]
        bias = _band_bias(rel_bias, tq_b, BAND_PAST + tq_b, BAND_PAST, lows, BAND_PAST + tq_b)
        o_b = _mixer_b(qb, kvb16, bias, tq=tq_b, tkb=tq_b, nkb=back + 1, back=back)
        conv_prev = jnp.zeros((b, CONV_W - 1, dff), F32)
    else:
        ck, cv, cki, cbk, cbv, conv_prev = cache
        past = ck.shape[1]
        rows_tail = t
        l_valid = past + t
        lp = -(-l_valid // LANES) * LANES
        tk_a = lp
        padk = jnp.zeros((b, lp - l_valid, W_A), BF16)
        k_all = jnp.concatenate([ck.reshape(b, past, W_A).astype(BF16), ka16, padk], axis=1)
        v_all = jnp.concatenate([cv.reshape(b, past, W_A).astype(BF16), va16, padk], axis=1)
        ki_all = jnp.concatenate([cki, ki, jnp.zeros((b, lp - l_valid, IDX_DIM), F32)], axis=1)
        ki_all = jnp.concatenate([ki_all, ki_all], axis=2)
        tq_a = -(-t // LANES) * LANES
        padq = lambda a: jnp.pad(a, ((0, 0), (0, tq_a - t), (0, 0)))
        o_a = _mixer_a(padq(qa), padq(qi), padq(wi), ki_all, k_all, v_all, past=past, valid_len=l_valid,
                       topk=min(TOPK_MAX, l_valid // 4), tq=tq_a, tk=tk_a)[:, :t]
        rows = cbk.shape[1]
        assert past % CHUNK == 0 and rows % CHUNK == 0
        lb_valid = rows + t
        lb = -(-lb_valid // LANES) * LANES
        kv_cache = jnp.concatenate([cbk.reshape(b, rows, W_B), cbv.reshape(b, rows, W_B)], axis=2).astype(BF16)
        kv_all = jnp.concatenate([kv_cache, kvb16, jnp.zeros((b, lb - lb_valid, 2 * W_B), BF16)], axis=1)
        bias = _band_bias(rel_bias, tq_a, lb, rows, [0], lb_valid)
        o_b = _mixer_b(padq(qb), kv_all, bias, tq=tq_a, tkb=lb, nkb=1, back=0)[:, :t]

    (tail32,) = _mm(h[:, t - rows_tail:].reshape(b * rows_tail, d), wts["kvb"], (F32,))
    tail32 = tail32.reshape(b, rows_tail, 2, N_HEADS_B, HEAD_DIM)
    new_bk, new_bv = tail32[:, :, 0], tail32[:, :, 1]

    merged = _merge(h2d, o_a.reshape(b * t, W_A), o_b.reshape(b * t, W_B), wts["pa"], wts["pb"],
                    wts["gate"], b_gate)
    x1, h2 = _outproj(merged.reshape(b, t, d), x, wts["out"], mod, norm2_g)
    y, tails = _ffn(h2, x1, mod, wts["up_a"], wts["up_b"], wts["down"], w_conv, b_conv,
                    conv_prev, final_g)
    conv_state = tails[:, -1]
    state = (ka32.reshape(b, t, N_HEADS_A, HEAD_DIM), va32.reshape(b, t, N_HEADS_A, HEAD_DIM), ki,
             new_bk, new_bv, conv_state)
    return y, state


def kernel(x_prompt, x_sample, cache_a_k, cache_a_v, cache_idx_k, cache_b_k, cache_b_v, state_ffn_conv,
           c_prompt, c_sample, w_ada, b_ada, norm1_g, w_in, rel_bias, w_gate, b_gate, w_proj_a, w_proj_b,
           w_out, norm2_g, w_up, w_conv, b_conv, w_down, final_g):
    depth = w_ada.shape[0]
    assert depth == 1, "the fused final RMSNorm assumes a single layer"
    d = x_prompt.shape[-1]
    nb_p = x_prompt.shape[0]
    c_all = jnp.concatenate([c_prompt, c_sample], axis=0)
    xp, xs = x_prompt, x_sample
    states_p, states_s = [], []
    for l in range(depth):
        wts = _prep_weights(w_in[l], w_gate[l], w_proj_a[l], w_proj_b[l], w_out[l], w_up[l], w_down[l])
        mod = _ada(c_all, w_ada[l], b_ada[l]).reshape(c_all.shape[0], 6, d)
        args = (wts, norm1_g[l], rel_bias[l], b_gate[l], norm2_g[l], w_conv[l], b_conv[l], final_g)
        xp, st_p = _trunk_layer(xp, mod[:nb_p], None, *args)
        states_p.append(st_p)
        cache_l = (cache_a_k[l], cache_a_v[l], cache_idx_k[l], cache_b_k[l], cache_b_v[l], state_ffn_conv[l])
        xs, st_s = _trunk_layer(xs, mod[nb_p:], cache_l, *args)
        states_s.append(st_s)
    sp = [t[0][None] for t in zip(*states_p)]
    ss = [t[0][None] for t in zip(*states_s)]
    return (xp, xs, *sp, *ss)
```

```python
import functools

import jax
import jax.numpy as jnp
import numpy as np
from jax import lax
from jax.experimental import pallas as pl
from jax.experimental.pallas import tpu as pltpu

F32 = jnp.float32
BF16 = jnp.bfloat16

CHUNK = 64
CHUNK_SHIFT = 6
HEAD_DIM = 128
N_HEADS_A = 8
N_IDX_HEADS = 16
IDX_DIM = 64
TOPK_MAX = 256
N_HEADS_B = 8
BAND_CHUNKS = 8
BAND_PAST = BAND_CHUNKS * CHUNK
REL_CLIP = 2 * CHUNK
CONV_W = 3
EPS = 1e-6
NEG = -1e30
W_A = N_HEADS_A * HEAD_DIM
W_B = N_HEADS_B * HEAD_DIM
ATTN_SCALE = HEAD_DIM ** -0.5
LOG2E = 1.4426950408889634
POS_UNBOUNDED = 2 ** 30
V_ROWS = HEAD_DIM + 16
HALF_BITS = 16
HALF_MASK = 0xFFFF
HALF_BIAS = 1 << (HALF_BITS - 1)
LANES = 128
MXU_COLS = 256
VMEM_LIMIT = 56 * 1024 * 1024
FFN_ACC_BYTES = 8 * 1024 * 1024


def _cp(*sem):
    return pltpu.CompilerParams(dimension_semantics=sem, vmem_limit_bytes=VMEM_LIMIT)


def _pick(dim, pref, align):
    t = min(pref, dim)
    t -= t % align
    while t >= align:
        if dim % t == 0:
            return t
        t -= align
    return dim


def _ada_kernel(c_ref, w_ref, b_ref, o_ref):
    c = c_ref[...]
    s = c * jax.nn.sigmoid(c)
    o_ref[...] = jnp.dot(s, w_ref[...], preferred_element_type=F32,
                         precision=lax.Precision.HIGHEST) + b_ref[...]


def _ada(c, w, b):
    r, d = c.shape
    n = w.shape[1]
    tn = _pick(n, 1024, LANES)
    return pl.pallas_call(
        _ada_kernel,
        grid=(n // tn,),
        in_specs=[pl.BlockSpec((r, d), lambda j: (0, 0)),
                  pl.BlockSpec((d, tn), lambda j: (0, j)),
                  pl.BlockSpec((1, tn), lambda j: (0, j))],
        out_specs=pl.BlockSpec((r, tn), lambda j: (0, j)),
        out_shape=jax.ShapeDtypeStruct((r, n), F32),
        compiler_params=_cp("arbitrary"),
        name="ada",
    )(c, w, b.reshape(1, n))


def _rms_mod(x, g, shift, scale):
    y = x * lax.rsqrt(jnp.mean(x * x, axis=-1, keepdims=True) + EPS) * g
    return y * (1.0 + scale) + shift


def _split_bf16(x):
    hi = x.astype(BF16)
    return hi, (x - hi.astype(F32)).astype(BF16)


def _norm_mod_kernel(x_ref, g_ref, mod_ref, hi_ref, lo_ref):
    h = _rms_mod(x_ref[0], g_ref[...], mod_ref[0, 0:1, :], mod_ref[0, 1:2, :])
    hi, lo = _split_bf16(h)
    hi_ref[0] = hi
    lo_ref[0] = lo


def _norm_mod(x, g, mod):
    b, t, d = x.shape
    tm = _pick(t, 512, 16)
    row = lambda i, j: (i, j, 0)
    return pl.pallas_call(
        _norm_mod_kernel,
        grid=(b, t // tm),
        in_specs=[pl.BlockSpec((1, tm, d), row),
                  pl.BlockSpec((1, d), lambda i, j: (0, 0)),
                  pl.BlockSpec((1, 6, d), lambda i, j: (i, 0, 0))],
        out_specs=[pl.BlockSpec((1, tm, d), row), pl.BlockSpec((1, tm, d), row)],
        out_shape=[jax.ShapeDtypeStruct((b, t, d), BF16), jax.ShapeDtypeStruct((b, t, d), BF16)],
        compiler_params=_cp("arbitrary", "arbitrary"),
        name="norm_mod",
    )(x, g.reshape(1, d), mod)


def _mm_kernel(a_ref, w_ref, *o_refs, scale):
    acc = jnp.dot(a_ref[...], w_ref[...], preferred_element_type=F32)
    if scale is not None:
        acc = acc * scale
    for o_ref in o_refs:
        o_ref[...] = acc.astype(o_ref.dtype).reshape(o_ref.shape)


def _mm(a, w, dtypes, scale=None, heads_out=()):
    m, k = a.shape
    n = w.shape[1]
    tm = _pick(m, 1024, 16)
    tn = _pick(n, 1024, LANES)
    nh = tn // HEAD_DIM
    specs = [pl.BlockSpec((tm, nh, HEAD_DIM), lambda i, j: (i, j, 0)) if o in heads_out
             else pl.BlockSpec((tm, tn), lambda i, j: (i, j)) for o in range(len(dtypes))]
    shapes = [jax.ShapeDtypeStruct((m, n // HEAD_DIM, HEAD_DIM) if o in heads_out else (m, n), dt)
              for o, dt in enumerate(dtypes)]
    outs = pl.pallas_call(
        functools.partial(_mm_kernel, scale=scale),
        grid=(m // tm, n // tn),
        in_specs=[pl.BlockSpec((tm, k), lambda i, j: (i, 0)),
                  pl.BlockSpec((k, tn), lambda i, j: (0, j))],
        out_specs=specs,
        out_shape=shapes,
        compiler_params=_cp("arbitrary", "arbitrary"),
        name="proj",
    )(a, w)
    return outs


def _mm_split_kernel(ah_ref, al_ref, wh_ref, wl_ref, o_ref):
    ah = ah_ref[...]
    acc = jnp.dot(ah, wh_ref[...], preferred_element_type=F32)
    acc = acc + jnp.dot(al_ref[...], wh_ref[...], preferred_element_type=F32)
    acc = acc + jnp.dot(ah, wl_ref[...], preferred_element_type=F32)
    o_ref[...] = acc


def _mm_split(a_hi, a_lo, w_hi, w_lo):
    m, k = a_hi.shape
    n = w_hi.shape[1]
    tm = _pick(m, 1024, 16)
    tn = _pick(n, 1024, LANES)
    a_spec = pl.BlockSpec((tm, k), lambda i, j: (i, 0))
    w_spec = pl.BlockSpec((k, tn), lambda i, j: (0, j))
    return pl.pallas_call(
        _mm_split_kernel,
        grid=(m // tm, n // tn),
        in_specs=[a_spec, a_spec, w_spec, w_spec],
        out_specs=pl.BlockSpec((tm, tn), lambda i, j: (i, j)),
        out_shape=jax.ShapeDtypeStruct((m, n), F32),
        compiler_params=_cp("arbitrary", "arbitrary"),
        name="proj_split",
    )(a_hi, a_lo, w_hi, w_lo)


def _mixer_a_kernel(tile_ref, block_ref, qa_ref, qi_ref, wit_ref, ki2_ref, k_ref, vt_ref, o_ref,
                    key_sc, hi_sc, lo_sc, thr_sc, cut_sc, tie_sc, m_sc, acc_sc, qx_sc,
                    *, tq, tk, past, valid_len, topk):
    i = tile_ref[pl.program_id(1)]
    j = block_ref[pl.program_id(1)]
    q0 = past + i * tq
    kend = jnp.minimum(((q0 + tq + CHUNK - 1) // CHUNK) * CHUNK, valid_len)
    nblk = (kend + tk - 1) // tk
    slab16 = 16

    def admissible(jb, row0=0, rows=tk):
        kpos = jb * tk + row0 + lax.broadcasted_iota(jnp.int32, (rows, tq), 0)
        qpos = q0 + lax.broadcasted_iota(jnp.int32, (rows, tq), 1)
        adm = (jnp.right_shift(kpos, CHUNK_SHIFT) <= jnp.right_shift(qpos, CHUNK_SHIFT)) & (kpos < valid_len)
        return adm, qpos, kpos

    @pl.when(j == 0)
    def _index_and_threshold():
        qt = qi_ref[0].T
        q_hi, q_lo = _split_bf16(qt)
        for h in range(N_IDX_HEADS):
            rows = slice(h * IDX_DIM, (h + 1) * IDX_DIM)
            qx_sc[h, 0 * IDX_DIM:1 * IDX_DIM, :] = q_hi[rows]
            qx_sc[h, 1 * IDX_DIM:2 * IDX_DIM, :] = q_hi[rows]
            qx_sc[h, 2 * IDX_DIM:3 * IDX_DIM, :] = q_lo[rows]
            qx_sc[h, 3 * IDX_DIM:4 * IDX_DIM, :] = q_lo[rows]


        nsplit = 2 if tk % (2 * MXU_COLS) == 0 else 1
        ts = tk // nsplit
        first = lax.broadcasted_iota(jnp.int32, (ts, LANES), 1) < IDX_DIM

        def score_block(jb, carry):
            for r in range(nsplit):
                rows = slice(r * ts, (r + 1) * ts)
                dup = ki2_ref[0, jb, rows, :]
                kk = jnp.where(first, dup, dup - dup.astype(BF16).astype(F32)).astype(BF16)
                kx = jnp.concatenate([kk, kk], axis=1)
                acc = jnp.zeros((ts, tq), F32)
                for h in range(N_IDX_HEADS):
                    rel = jnp.dot(kx, qx_sc[h], preferred_element_type=F32)
                    acc = acc + wit_ref[0, h:h + 1, :] * jnp.maximum(rel, 0.0)
                acc = jnp.where(admissible(jb, r * ts, ts)[0], acc, NEG)
                bits = pltpu.bitcast(acc, jnp.int32)
                key = bits ^ (jnp.right_shift(bits, 31) & 0x7FFFFFFF)
                key_sc[jb, rows, :] = key
                hi_sc[jb, rows, :] = jnp.right_shift(key, HALF_BITS).astype(jnp.int16)
                lo_sc[jb, rows, :] = ((key & HALF_MASK) - HALF_BIAS).astype(jnp.int16)
            return carry

        lax.fori_loop(0, nblk, score_block, 0)

        def count_ge(src_sc, cand):
            cand16 = cand.astype(jnp.int16)

            def count_block(jb, cnt):
                part = [None] * 4
                for r in range(tk // slab16):
                    rows = slice(r * slab16, (r + 1) * slab16)
                    hit = jnp.where(src_sc[jb, rows, :] >= cand16, jnp.int16(1), jnp.int16(0))
                    part[r % 4] = hit if part[r % 4] is None else part[r % 4] + hit
                return cnt + ((part[0] + part[1]) + (part[2] + part[3]))

            cnt = lax.fori_loop(0, nblk, count_block, jnp.zeros((slab16, tq), jnp.int16))
            return jnp.sum(cnt.astype(jnp.int32), axis=0, keepdims=True)

        def kth_largest16(src_sc, want, n_all):
            def bit_step(b, state):
                t_u, n_t = state
                cand_u = t_u | lax.shift_left(jnp.int32(1), HALF_BITS - 1 - b)
                n_cand = count_ge(src_sc, cand_u - HALF_BIAS)
                take = n_cand >= want
                return jnp.where(take, cand_u, t_u), jnp.where(take, n_cand, n_t)
            t_u, n_t = lax.fori_loop(0, HALF_BITS, bit_step, (jnp.zeros((slab16, tq), jnp.int32), n_all))
            return t_u - HALF_BIAS, n_t

        t_hi, n_hi = kth_largest16(hi_sc, topk, jnp.broadcast_to(nblk * tk, (1, tq)))
        above = jnp.where(t_hi == HALF_BIAS - 1, 0, count_ge(hi_sc, jnp.minimum(t_hi + 1, HALF_BIAS - 1)))
        t_hi16 = t_hi.astype(jnp.int16)

        def keep_group(jb, carry):
            for r in range(tk // slab16):
                rows = slice(r * slab16, (r + 1) * slab16)
                lo_sc[jb, rows, :] = jnp.where(hi_sc[jb, rows, :] == t_hi16, lo_sc[jb, rows, :],
                                               jnp.int16(-HALF_BIAS))
            return carry

        lax.fori_loop(0, nblk, keep_group, 0)
        above = above[0:1, :]
        t_lo, n_lo = kth_largest16(lo_sc, topk - above, n_hi - above)
        thr = (t_hi * (2 * HALF_BIAS) + (t_lo + HALF_BIAS))[0:1, :]
        thr_sc[...] = jnp.broadcast_to(thr, thr_sc.shape)

        def count_keys(pred):
            def count_block(jb, cnt):
                hit = pred(key_sc[jb], admissible(jb)[2])
                return cnt + jnp.sum(jnp.where(hit, 1, 0), axis=0, keepdims=True)
            return lax.fori_loop(0, nblk, count_block, jnp.zeros((1, tq), jnp.int32))

        surplus = jnp.max(above + n_lo) - topk
        tie_sc[0] = surplus
        cut_sc[...] = jnp.full(cut_sc.shape, POS_UNBOUNDED, jnp.int32)

        @pl.when(surplus > 0)
        def _break_ties():
            keep = topk - count_keys(lambda key, kpos: key > thr)
            pos_bits = (key_sc.shape[0] * tk).bit_length()

            def bit_step(b, x):
                cand = x | lax.shift_left(jnp.int32(1), pos_bits - 1 - b)
                below = count_keys(lambda key, kpos: (key == thr) & (kpos < cand))
                return jnp.where(below <= keep, cand, x)

            x = lax.fori_loop(0, pos_bits, bit_step, jnp.zeros((1, tq), jnp.int32))
            cut_sc[...] = jnp.broadcast_to(x, cut_sc.shape)

        m_sc[...] = jnp.full(m_sc.shape, -jnp.inf, F32)
        acc_sc[...] = jnp.zeros(acc_sc.shape, F32)

    def _attend(select):
        adm, qpos, kpos = admissible(j)
        dist = jnp.where(select(key_sc[j], kpos) & adm, jnp.abs(qpos - kpos).astype(F32), -NEG)

        nch = 2 if tk % (2 * MXU_COLS) == 0 else 1
        ck = tk // nch

        def logits(h, c):
            hs = slice(h * HEAD_DIM, (h + 1) * HEAD_DIM)
            return lax.dot_general(k_ref[0, c * ck:(c + 1) * ck, hs], qa_ref[0, :, hs],
                                   (((1,), (1,)), ((), ())), preferred_element_type=F32)

        s_next = [logits(0, c) for c in range(nch)]
        for h in range(N_HEADS_A):
            hx = slice(h * V_ROWS, (h + 1) * V_ROWS)
            s_raw, s_next = s_next, []
            m_prev = m_sc[h:h + 1, :]
            m_new = m_prev
            s = []
            for c in range(nch):
                if h + 1 < N_HEADS_A:
                    s_next.append(logits(h + 1, c))
                s.append(s_raw[c] - ((2.0 ** -(h + 1)) * LOG2E) * dist[c * ck:(c + 1) * ck])
                m_new = jnp.maximum(m_new, jnp.max(s[c], axis=0, keepdims=True))
            acc = jnp.exp2(m_prev - m_new) * acc_sc[hx, :]
            for c in range(nch):
                p = jnp.exp2(s[c] - m_new)
                acc = acc + jnp.dot(vt_ref[0, hx, c * ck:(c + 1) * ck], p.astype(BF16),
                                    preferred_element_type=F32)
            acc_sc[hx, :] = acc
            m_sc[h:h + 1, :] = m_new

    @pl.when(tie_sc[0] <= 0)
    def _():
        _attend(lambda key, kpos: key >= thr_sc[0:1, :])

    @pl.when(tie_sc[0] > 0)
    def _():
        _attend(lambda key, kpos: (key > thr_sc[0:1, :])
                | ((key == thr_sc[0:1, :]) & (kpos < cut_sc[0:1, :])))

    @pl.when(j == nblk - 1)
    def _finalize():
        for h in range(N_HEADS_A):
            hs = slice(h * HEAD_DIM, (h + 1) * HEAD_DIM)
            num = acc_sc[h * V_ROWS:h * V_ROWS + HEAD_DIM, :]
            den = acc_sc[h * V_ROWS + HEAD_DIM:h * V_ROWS + HEAD_DIM + 1, :]
            o_ref[0, :, hs] = (num / den).T.astype(o_ref.dtype)


def _mixer_a(q2, qi, wi, ki2, k_all, v_all, *, past, valid_len, topk, tq, tk):
    b, t, _ = q2.shape
    lp = k_all.shape[1]
    nkb = lp // tk
    assert lp % tk == 0 and t % tq == 0 and tk % LANES == 0 and past % CHUNK == 0
    ki2 = ki2.reshape(b, nkb, tk, ki2.shape[2])
    wit = jnp.swapaxes(wi, 1, 2)
    vt = jnp.swapaxes(v_all, 1, 2).reshape(b, N_HEADS_A, HEAD_DIM, lp)
    vt = jnp.concatenate([vt, jnp.ones((b, N_HEADS_A, V_ROWS - HEAD_DIM, lp), BF16)], axis=2)
    vt = vt.reshape(b, N_HEADS_A * V_ROWS, lp)

    def blocks_needed(i):
        kend = min(-(-(past + (i + 1) * tq) // CHUNK) * CHUNK, valid_len)
        return -(-kend // tk)

    pairs = [(i, j) for i in range(t // tq) for j in range(blocks_needed(i))]
    tile_of = jnp.asarray([p[0] for p in pairs], jnp.int32)
    block_of = jnp.asarray([p[1] for p in pairs], jnp.int32)
    q_map = lambda bb, p, ti, bj: (bb, ti[p], 0)
    kernel = functools.partial(_mixer_a_kernel, tq=tq, tk=tk, past=past, valid_len=valid_len, topk=topk)
    grid_spec = pltpu.PrefetchScalarGridSpec(
        num_scalar_prefetch=2,
        grid=(b, len(pairs)),
        in_specs=[pl.BlockSpec((1, tq, W_A), q_map),
                  pl.BlockSpec((1, tq, N_IDX_HEADS * IDX_DIM), q_map),
                  pl.BlockSpec((1, N_IDX_HEADS, tq), lambda bb, p, ti, bj: (bb, 0, ti[p])),
                  pl.BlockSpec((1, nkb, tk, 2 * IDX_DIM), lambda bb, p, ti, bj: (bb, 0, 0, 0)),
                  pl.BlockSpec((1, tk, W_A), lambda bb, p, ti, bj: (bb, bj[p], 0)),
                  pl.BlockSpec((1, N_HEADS_A * V_ROWS, tk), lambda bb, p, ti, bj: (bb, 0, bj[p]))],
        out_specs=pl.BlockSpec((1, tq, W_A), q_map),
        scratch_shapes=[pltpu.VMEM((nkb, tk, tq), jnp.int32),
                        pltpu.VMEM((nkb, tk, tq), jnp.int16),
                        pltpu.VMEM((nkb, tk, tq), jnp.int16),
                        pltpu.VMEM((8, tq), jnp.int32),
                        pltpu.VMEM((8, tq), jnp.int32),
                        pltpu.SMEM((1,), jnp.int32),
                        pltpu.VMEM((N_HEADS_A, tq), F32),
                        pltpu.VMEM((N_HEADS_A * V_ROWS, tq), F32),
                        pltpu.VMEM((N_IDX_HEADS, 4 * IDX_DIM, tq), BF16)])
    return pl.pallas_call(
        kernel,
        grid_spec=grid_spec,
        out_shape=jax.ShapeDtypeStruct((b, t, W_A), BF16),
        compiler_params=_cp("arbitrary", "arbitrary"),
        name="mixer_a",
    )(tile_of, block_of, q2, qi, wit, ki2, k_all, vt)


def _mixer_b_kernel(*refs, nkb, tkb):
    q_ref = refs[0]
    k_refs = refs[1:1 + nkb]
    v_refs = refs[1 + nkb:1 + 2 * nkb]
    bias_ref = refs[1 + 2 * nkb]
    o_ref = refs[2 + 2 * nkb]
    def logits(h, c):
        hs = slice(h * HEAD_DIM, (h + 1) * HEAD_DIM)
        return lax.dot_general(k_refs[c][0, :, hs], q_ref[0, :, hs], (((1,), (1,)), ((), ())),
                               preferred_element_type=F32)

    s_next = [logits(0, c) for c in range(nkb)]
    for h in range(N_HEADS_B):
        hs = slice(h * HEAD_DIM, (h + 1) * HEAD_DIM)
        s_raw, s_next = s_next, []
        s = []
        m = None
        for c in range(nkb):
            if h + 1 < N_HEADS_B:
                s_next.append(logits(h + 1, c))
            s.append(s_raw[c] * (ATTN_SCALE * LOG2E) + bias_ref[0, h, c * tkb:(c + 1) * tkb, :])
            mc = jnp.max(s[c], axis=0, keepdims=True)
            m = mc if m is None else jnp.maximum(m, mc)
        l = None
        pv = None
        for c, vr in enumerate(v_refs):
            p = jnp.exp2(s[c] - m)
            lc = jnp.sum(p, axis=0, keepdims=True)
            t = lax.dot_general(vr[0, :, hs], p.astype(BF16), (((0,), (0,)), ((), ())),
                                preferred_element_type=F32)
            l = lc if l is None else l + lc
            pv = t if pv is None else pv + t
        o_ref[0, :, hs] = (pv / l).T.astype(o_ref.dtype)


def _band_bias(rel_bias, tq, wk, off, lows, hi):
    nh = rel_bias.shape[0]
    n = wk + tq
    e = np.concatenate([np.arange(wk + 1), np.arange(-(tq - 1), 0)])
    f = rel_bias[:, np.clip(off - e, -REL_CLIP, REL_CLIP) + REL_CLIP].astype(F32) * LOG2E
    bias = jnp.tile(f, (1, tq))[:, :tq * (n - 1)].reshape(nh, tq, n - 1)[:, :, :wk]
    bias = jnp.swapaxes(bias, 1, 2)
    c = np.arange(wk)[:, None]
    r = np.arange(tq)[None, :]
    dq = r // CHUNK
    dk = np.floor_divide(c - off, CHUNK)
    ok = (dk <= dq) & (dk >= dq - BAND_CHUNKS) & (c < hi)
    tiles = [jnp.where((ok & (c >= lo))[None], bias, NEG) for lo in lows]
    return jnp.stack(tiles)


def _mixer_b(q2, kv, bias, *, tq, tkb, nkb, back):
    b, t, _ = q2.shape
    nvar = bias.shape[0]
    wk = nkb * tkb
    kmaps = [functools.partial(lambda bb, i, p, col: (bb, jnp.maximum(i - back + p, 0), col), p=p, col=0)
             for p in range(nkb)]
    vmaps = [functools.partial(lambda bb, i, p, col: (bb, jnp.maximum(i - back + p, 0), col), p=p, col=1)
             for p in range(nkb)]
    kernel = functools.partial(_mixer_b_kernel, nkb=nkb, tkb=tkb)
    return pl.pallas_call(
        kernel,
        grid=(b, t // tq),
        in_specs=([pl.BlockSpec((1, tq, W_B), lambda bb, i: (bb, i, 0))]
                  + [pl.BlockSpec((1, tkb, W_B), m) for m in kmaps]
                  + [pl.BlockSpec((1, tkb, W_B), m) for m in vmaps]
                  + [pl.BlockSpec((1, N_HEADS_B, wk, tq), lambda bb, i: (jnp.minimum(i, nvar - 1), 0, 0, 0))]),
        out_specs=pl.BlockSpec((1, tq, W_B), lambda bb, i: (bb, i, 0)),
        out_shape=jax.ShapeDtypeStruct((b, t, W_B), BF16),
        compiler_params=_cp("arbitrary", "arbitrary"),
        name="mixer_b",
    )(q2, *([kv] * (2 * nkb)), bias)


def _merge_kernel(h_ref, oa_ref, ob_ref, pa_ref, pb_ref, wga_ref, wgb_ref, bga_ref, bgb_ref, o_ref):
    h = h_ref[...]
    ga = jax.nn.sigmoid(jnp.dot(h, wga_ref[...], preferred_element_type=F32) + bga_ref[...])
    gb = jax.nn.sigmoid(jnp.dot(h, wgb_ref[...], preferred_element_type=F32) + bgb_ref[...])
    ya = jnp.dot(oa_ref[...], pa_ref[...], preferred_element_type=F32)
    yb = jnp.dot(ob_ref[...], pb_ref[...], preferred_element_type=F32)
    o_ref[...] = (ga * ya + gb * yb).astype(o_ref.dtype)


def _merge(h, oa, ob, pa, pb, wg, bg):
    m, d = h.shape
    tm = _pick(m, 512, 16)
    tn = _pick(d, 512, LANES)
    nb = d // tn
    bg2 = bg.reshape(1, 2 * d)
    return pl.pallas_call(
        _merge_kernel,
        grid=(m // tm, nb),
        in_specs=[pl.BlockSpec((tm, d), lambda i, j: (i, 0)),
                  pl.BlockSpec((tm, W_A), lambda i, j: (i, 0)),
                  pl.BlockSpec((tm, W_B), lambda i, j: (i, 0)),
                  pl.BlockSpec((W_A, tn), lambda i, j: (0, j)),
                  pl.BlockSpec((W_B, tn), lambda i, j: (0, j)),
                  pl.BlockSpec((d, tn), lambda i, j: (0, j)),
                  pl.BlockSpec((d, tn), lambda i, j: (0, j + nb)),
                  pl.BlockSpec((1, tn), lambda i, j: (0, j)),
                  pl.BlockSpec((1, tn), lambda i, j: (0, j + nb))],
        out_specs=pl.BlockSpec((tm, tn), lambda i, j: (i, j)),
        out_shape=jax.ShapeDtypeStruct((m, d), BF16),
        compiler_params=_cp("arbitrary", "arbitrary"),
        name="merge",
    )(h, oa, ob, pa, pb, wg, wg, bg2, bg2)


def _outproj_kernel(mg_ref, x_ref, w_ref, mod_ref, g_ref, x1_ref, h2_ref):
    o = jnp.dot(mg_ref[0], w_ref[...], preferred_element_type=F32)
    x1 = x_ref[0] + mod_ref[0, 2:3, :] * o
    x1_ref[0] = x1
    h2_ref[0] = _rms_mod(x1, g_ref[...], mod_ref[0, 3:4, :], mod_ref[0, 4:5, :]).astype(h2_ref.dtype)


def _outproj(merged, x, w_out, mod, g2):
    b, t, d = x.shape
    tm = _pick(t, 256, 16)
    row = lambda i, j: (i, j, 0)
    return pl.pallas_call(
        _outproj_kernel,
        grid=(b, t // tm),
        in_specs=[pl.BlockSpec((1, tm, d), row),
                  pl.BlockSpec((1, tm, d), row),
                  pl.BlockSpec((d, d), lambda i, j: (0, 0)),
                  pl.BlockSpec((1, 6, d), lambda i, j: (i, 0, 0)),
                  pl.BlockSpec((1, d), lambda i, j: (0, 0))],
        out_specs=[pl.BlockSpec((1, tm, d), row), pl.BlockSpec((1, tm, d), row)],
        out_shape=[jax.ShapeDtypeStruct((b, t, d), F32), jax.ShapeDtypeStruct((b, t, d), BF16)],
        compiler_params=_cp("arbitrary", "arbitrary"),
        name="outproj",
    )(merged, x, w_out, mod, g2.reshape(1, d))


def _ffn_kernel(h2_ref, x1_ref, mod_ref, wa_ref, wb_ref, wd_ref, wc_ref, bc_ref, prev_ref, fg_ref,
                y_ref, st_ref, abuf, carry, acc_sc, *, tm, nf, weights_outer):
    if weights_outer:
        f, slot, m = pl.program_id(0), pl.program_id(1), pl.program_id(2)
    else:
        slot, m, f = 0, pl.program_id(1), pl.program_id(2)

    @pl.when(f == 0)
    def _():
        acc_sc[slot] = jnp.zeros(acc_sc.shape[1:], F32)

    h2 = h2_ref[0]
    prev = jnp.where(m == 0, prev_ref[0], carry[f])
    tf = wa_ref.shape[1]
    nsub = 2 if tf % (2 * MXU_COLS) == 0 else 1
    tfs = tf // nsub

    def up(c):
        cols = slice(c * tfs, (c + 1) * tfs)
        return (jnp.dot(h2, wa_ref[:, cols], preferred_element_type=F32),
                jnp.dot(h2, wb_ref[:, cols], preferred_element_type=F32))

    nxt = up(0)
    for c in range(nsub):
        cols = slice(c * tfs, (c + 1) * tfs)
        a, gate = nxt
        if c + 1 < nsub:
            nxt = up(c + 1)
        abuf[6:8, cols] = prev[:, cols]
        abuf[8:8 + tm, cols] = a
        tail = a[tm - 2:tm, :]
        carry[f, :, cols] = tail
        st_ref[0, 0, :, cols] = tail
        conv = (bc_ref[:, cols] + wc_ref[0:1, cols] * abuf[6:6 + tm, cols]
                + wc_ref[1:2, cols] * abuf[7:7 + tm, cols] + wc_ref[2:3, cols] * a)
        u = (conv * jax.nn.sigmoid(conv) * gate).astype(BF16)
        acc_sc[slot] += jnp.dot(u, wd_ref[cols, :], preferred_element_type=F32)

    @pl.when(f == nf - 1)
    def _():
        x2 = x1_ref[0] + mod_ref[0, 5:6, :] * acc_sc[slot]
        y_ref[0] = x2 * lax.rsqrt(jnp.mean(x2 * x2, axis=-1, keepdims=True) + EPS) * fg_ref[...]


def _ffn(h2, x1, mod, wa, wb, wd, wconv, bconv, prev, final_g):
    b, t, d = x1.shape
    dff = wa.shape[1]
    tm = _pick(t, 512, 16)
    tf = _pick(dff, 512, LANES)
    nf = dff // tf
    nm = t // tm
    weights_outer = nm == 1 and b * tm * d * 4 <= FFN_ACC_BYTES
    if weights_outer:
        grid = (nf, b, nm)
        spec = lambda shape, fn: pl.BlockSpec(shape, lambda k, i, j: fn(i, j, k))
        row = lambda i, j, k: (jnp.where(k == nf - 1, i, 0), jnp.where(k == nf - 1, j, 0), 0)
    else:
        grid = (b, nm, nf)
        spec = lambda shape, fn: pl.BlockSpec(shape, fn)
        row = lambda i, j, k: (i, j, 0)
    kernel = functools.partial(_ffn_kernel, tm=tm, nf=nf, weights_outer=weights_outer)
    return pl.pallas_call(
        kernel,
        grid=grid,
        in_specs=[spec((1, tm, d), lambda i, j, k: (i, j, 0)),
                  spec((1, tm, d), row),
                  spec((1, 6, d), lambda i, j, k: (i, 0, 0)),
                  spec((d, tf), lambda i, j, k: (0, k)),
                  spec((d, tf), lambda i, j, k: (0, k)),
                  spec((tf, d), lambda i, j, k: (k, 0)),
                  spec((CONV_W, tf), lambda i, j, k: (0, k)),
                  spec((1, tf), lambda i, j, k: (0, k)),
                  spec((1, CONV_W - 1, tf), lambda i, j, k: (i, 0, k)),
                  spec((1, d), lambda i, j, k: (0, 0))],
        out_specs=[spec((1, tm, d), row),
                   spec((1, 1, CONV_W - 1, tf), lambda i, j, k: (i, j, 0, k))],
        out_shape=[jax.ShapeDtypeStruct((b, t, d), F32),
                   jax.ShapeDtypeStruct((b, nm, CONV_W - 1, dff), F32)],
        scratch_shapes=[pltpu.VMEM((tm + 8, tf), F32),
                        pltpu.VMEM((nf, CONV_W - 1, tf), F32),
                        pltpu.VMEM((b if weights_outer else 1, tm, d), F32)],
        compiler_params=_cp("arbitrary", "arbitrary", "arbitrary"),
        name="ffn",
    )(h2, x1, mod, wa, wb, wd, wconv, bconv.reshape(1, dff), prev, final_g.reshape(1, d))


def _prep_weights(w_in, w_gate, w_proj_a, w_proj_b, w_out, w_up, w_down):
    o = np.cumsum((0, W_A, W_A, W_A, N_IDX_HEADS * IDX_DIM, IDX_DIM, N_IDX_HEADS, W_B, W_B, W_B))
    col = lambda a, b: w_in[:, int(o[a]):int(o[b])]
    d = w_in.shape[0]
    dff = w_down.shape[0]
    pad = jnp.zeros((d, MXU_COLS - 2 * IDX_DIM - N_IDX_HEADS), w_in.dtype)
    qi_hi, qi_lo = _split_bf16(col(3, 4))
    kiwi_hi, kiwi_lo = _split_bf16(jnp.concatenate([col(4, 5), col(4, 5), col(5, 6), pad], axis=1))
    return dict(
        qa=col(0, 1).astype(BF16),
        qb=col(6, 7).astype(BF16),
        ka=col(1, 2).astype(BF16),
        va=col(2, 3).astype(BF16),
        qi_hi=qi_hi, qi_lo=qi_lo, kiwi_hi=kiwi_hi, kiwi_lo=kiwi_lo,
        kvb=col(7, 9).astype(BF16),
        gate=w_gate.astype(BF16),
        pa=w_proj_a.astype(BF16),
        pb=w_proj_b.astype(BF16),
        out=w_out.astype(BF16),
        up_a=w_up[:, :dff].astype(BF16),
        up_b=w_up[:, dff:].astype(BF16),
        down=w_down.astype(BF16),
    )


def _trunk_layer(x, mod, cache, wts, norm1_g, rel_bias, b_gate, norm2_g, w_conv, b_conv, final_g):
    b, t, d = x.shape
    dff = wts["down"].shape[0]
    h, h_lo = _norm_mod(x, norm1_g, mod)
    h2d = h.reshape(b * t, d)
    hl2d = h_lo.reshape(b * t, d)
    (qa,) = _mm(h2d, wts["qa"], (BF16,), scale=ATTN_SCALE * LOG2E)
    (qb,) = _mm(h2d, wts["qb"], (BF16,))
    ka32, ka16 = _mm(h2d, wts["ka"], (F32, BF16), heads_out=(0,))
    va32, va16 = _mm(h2d, wts["va"], (F32, BF16), heads_out=(0,))
    qi = _mm_split(h2d, hl2d, wts["qi_hi"], wts["qi_lo"]).reshape(b, t, N_IDX_HEADS * IDX_DIM)
    kiwi = _mm_split(h2d, hl2d, wts["kiwi_hi"], wts["kiwi_lo"])
    (kvb16,) = _mm(h2d, wts["kvb"], (BF16,))
    qa = qa.reshape(b, t, W_A)
    qb = qb.reshape(b, t, W_B)
    kiwi = kiwi.reshape(b, t, MXU_COLS)
    ki = kiwi[:, :, :IDX_DIM]
    wi = kiwi[:, :, 2 * IDX_DIM:2 * IDX_DIM + N_IDX_HEADS]
    ka16 = ka16.reshape(b, t, W_A)
    va16 = va16.reshape(b, t, W_A)
    kvb16 = kvb16.reshape(b, t, 2 * W_B)

    if cache is None:
        rows_tail = min(BAND_PAST, t)
        tq_a = _pick(t, 256, CHUNK)
        tk_a = _pick(t, 512, LANES)
        o_a = _mixer_a(qa, qi, wi, kiwi, ka16, va16, past=0, valid_len=t, topk=min(TOPK_MAX, t // 4),
                       tq=tq_a, tk=tk_a)
        tq_b = _pick(t, 256, CHUNK)
        assert BAND_PAST % tq_b == 0
        back = BAND_PAST // tq_b
        lows = [BAND_PAST - v * tq_b for v in range(back + 1)]
        bias = _band_bias(rel_bias, tq_b, BAND_PAST + tq_b, BAND_PAST, lows, BAND_PAST + tq_b)
        o_b = _mixer_b(qb, kvb16, bias, tq=tq_b, tkb=tq_b, nkb=back + 1, back=back)
        conv_prev = jnp.zeros((b, CONV_W - 1, dff), F32)
    else:
        ck, cv, cki, cbk, cbv, conv_prev = cache
        past = ck.shape[1]
        rows_tail = t
        l_valid = past + t
        lp = -(-l_valid // LANES) * LANES
        tk_a = lp
        padk = jnp.zeros((b, lp - l_valid, W_A), BF16)
        k_all = jnp.concatenate([ck.reshape(b, past, W_A).astype(BF16), ka16, padk], axis=1)
        v_all = jnp.concatenate([cv.reshape(b, past, W_A).astype(BF16), va16, padk], axis=1)
        ki_all = jnp.concatenate([cki, ki, jnp.zeros((b, lp - l_valid, IDX_DIM), F32)], axis=1)
        ki_all = jnp.concatenate([ki_all, ki_all], axis=2)
        tq_a = -(-t // LANES) * LANES
        padq = lambda a: jnp.pad(a, ((0, 0), (0, tq_a - t), (0, 0)))
        o_a = _mixer_a(padq(qa), padq(qi), padq(wi), ki_all, k_all, v_all, past=past, valid_len=l_valid,
                       topk=min(TOPK_MAX, l_valid // 4), tq=tq_a, tk=tk_a)[:, :t]
        rows = cbk.shape[1]
        assert past % CHUNK == 0 and rows % CHUNK == 0
        lb_valid = rows + t
        lb = -(-lb_valid // LANES) * LANES
        kv_cache = jnp.concatenate([cbk.reshape(b, rows, W_B), cbv.reshape(b, rows, W_B)], axis=2).astype(BF16)
        kv_all = jnp.concatenate([kv_cache, kvb16, jnp.zeros((b, lb - lb_valid, 2 * W_B), BF16)], axis=1)
        bias = _band_bias(rel_bias, tq_a, lb, rows, [0], lb_valid)
        o_b = _mixer_b(padq(qb), kv_all, bias, tq=tq_a, tkb=lb, nkb=1, back=0)[:, :t]

    (tail32,) = _mm(h[:, t - rows_tail:].reshape(b * rows_tail, d), wts["kvb"], (F32,))
    tail32 = tail32.reshape(b, rows_tail, 2, N_HEADS_B, HEAD_DIM)
    new_bk, new_bv = tail32[:, :, 0], tail32[:, :, 1]

    merged = _merge(h2d, o_a.reshape(b * t, W_A), o_b.reshape(b * t, W_B), wts["pa"], wts["pb"],
                    wts["gate"], b_gate)
    x1, h2 = _outproj(merged.reshape(b, t, d), x, wts["out"], mod, norm2_g)
    y, tails = _ffn(h2, x1, mod, wts["up_a"], wts["up_b"], wts["down"], w_conv, b_conv,
                    conv_prev, final_g)
    conv_state = tails[:, -1]
    state = (ka32.reshape(b, t, N_HEADS_A, HEAD_DIM), va32.reshape(b, t, N_HEADS_A, HEAD_DIM), ki,
             new_bk, new_bv, conv_state)
    return y, state


def kernel(x_prompt, x_sample, cache_a_k, cache_a_v, cache_idx_k, cache_b_k, cache_b_v, state_ffn_conv,
           c_prompt, c_sample, w_ada, b_ada, norm1_g, w_in, rel_bias, w_gate, b_gate, w_proj_a, w_proj_b,
           w_out, norm2_g, w_up, w_conv, b_conv, w_down, final_g):
    depth = w_ada.shape[0]
    assert depth == 1, "the fused final RMSNorm assumes a single layer"
    d = x_prompt.shape[-1]
    nb_p = x_prompt.shape[0]
    c_all = jnp.concatenate([c_prompt, c_sample], axis=0)
    xp, xs = x_prompt, x_sample
    states_p, states_s = [], []
    for l in range(depth):
        wts = _prep_weights(w_in[l], w_gate[l], w_proj_a[l], w_proj_b[l], w_out[l], w_up[l], w_down[l])
        mod = _ada(c_all, w_ada[l], b_ada[l]).reshape(c_all.shape[0], 6, d)
        args = (wts, norm1_g[l], rel_bias[l], b_gate[l], norm2_g[l], w_conv[l], b_conv[l], final_g)
        xp, st_p = _trunk_layer(xp, mod[:nb_p], None, *args)
        states_p.append(st_p)
        cache_l = (cache_a_k[l], cache_a_v[l], cache_idx_k[l], cache_b_k[l], cache_b_v[l], state_ffn_conv[l])
        xs, st_s = _trunk_layer(xs, mod[nb_p:], cache_l, *args)
        states_s.append(st_s)
    sp = [t[0][None] for t in zip(*states_p)]
    ss = [t[0][None] for t in zip(*states_s)]
    return (xp, xs, *sp, *ss)
```
